```python
import jax, jax.numpy as jnp
from jax import lax
import numpy as np

D_MODEL = 1024
BATCH = 1
SEQ = 16384
DEPTH = 1
DEC_BATCH = 32
DEC_SEQ = 4
PAST_LEN = 16384
PAGE_SIZE = 128

MIX_WIDTH = D_MODEL
W_M = MIX_WIDTH // 2
W_F = MIX_WIDTH - W_M
H_M = 4
D_HM = W_M // H_M
H_F = 8
D_HF = W_F // H_F
CHUNK = 128
Q_BLOCK = 128
FOX_SCALE = D_HF ** -0.5
N_GROUPS = 4
EXP_PER_GROUP = 4
TOP_K_IN_GROUP = 2
D_FF_E = D_MODEL // 4
PROJ_WIDTH = 4 * W_M + 2 * H_M + 3 * W_F + H_F
EPS = 1e-6

kernel_name = 'hymba_mlstm_fox_hmoe_step'


def rms_norm(x, g):
    x32 = x.astype(jnp.float32)
    y = x32 * lax.rsqrt(jnp.mean(x32 * x32, axis=-1, keepdims=True) + EPS)
    return (y * g.astype(jnp.float32)).astype(x.dtype)


def adaln_params(c, w_ada, b_ada):
    mod = jax.nn.silu(c.astype(jnp.float32)) @ w_ada.astype(jnp.float32) + b_ada.astype(jnp.float32)
    return jnp.split(mod[:, None, :], 6, axis=-1)


def modulate(x, g, shift, scale):
    h = rms_norm(x, g).astype(jnp.float32) * (1.0 + scale) + shift
    return h.astype(x.dtype)


def mixer_inputs(h, w_in, b_ig, b_fg, b_ff):
    B, T = h.shape[:2]
    p = jnp.einsum('btd,dp->btp', h, w_in).astype(jnp.float32)
    sizes = [W_M, W_M, W_M, W_M, H_M, H_M, W_F, W_F, W_F, H_F]
    mq, mk, mv, mo, mi, mf, fq, fk, fv, ff = jnp.split(p, np.cumsum(sizes)[:-1].tolist(), axis=-1)
    mq = mq.reshape(B, T, H_M, D_HM)
    mk = mk.reshape(B, T, H_M, D_HM) * (D_HM ** -0.5)
    mv = mv.reshape(B, T, H_M, D_HM)
    mo = jax.nn.sigmoid(mo).reshape(B, T, H_M, D_HM)
    mi = mi + b_ig.astype(jnp.float32)
    mlf = jax.nn.log_sigmoid(mf + b_fg.astype(jnp.float32))
    fq = fq.reshape(B, T, H_F, D_HF)
    fk = fk.reshape(B, T, H_F, D_HF)
    fv = fv.reshape(B, T, H_F, D_HF)
    flf = jax.nn.log_sigmoid(ff + b_ff.astype(jnp.float32))
    return mq, mk, mv, mo, mi, mlf, fq, fk, fv, flf


def mlstm_chunk(state, inp):
    C, n, m = state
    q, k, v, ig, lf = inp
    L = q.shape[2]
    b = jnp.cumsum(lf, axis=-1)
    causal = jnp.tril(jnp.ones((L, L), dtype=bool))
    dmat = jnp.where(causal, b[..., :, None] - b[..., None, :] + ig[..., None, :], -jnp.inf)
    inter = b + m[..., None]
    m_t = jnp.maximum(inter, jnp.max(dmat, axis=-1))
    w_intra = jnp.exp(dmat - m_t[..., None])
    w_inter = jnp.exp(inter - m_t)
    s = jnp.einsum('bhtd,bhsd->bhts', q, k) * w_intra
    num = jnp.einsum('bhts,bhsd->bhtd', s, v) + w_inter[..., None] * jnp.einsum('bhvk,bhtk->bhtv', C, q)
    den = jnp.sum(s, axis=-1) + w_inter * jnp.einsum('bhk,bhtk->bht', n, q)
    h = num / jnp.maximum(jnp.abs(den), jnp.exp(-m_t))[..., None]
    m_new = m_t[..., -1]
    w_state = jnp.exp(b[..., -1:] - b + ig - m_new[..., None])
    decay = jnp.exp(b[..., -1] + m - m_new)
    C_new = decay[..., None, None] * C + jnp.einsum('bhs,bhsv,bhsk->bhvk', w_state, v, k)
    n_new = decay[..., None] * n + jnp.einsum('bhs,bhsk->bhk', w_state, k)
    return (C_new, n_new, m_new), h


def mlstm_sequence(q, k, v, ig, lf, C0, n0, m0, chunk):
    B, T = q.shape[:2]
    nc = T // chunk

    def chunks(a):
        a = jnp.swapaxes(a, 1, 2)
        a = a.reshape(a.shape[:2] + (nc, chunk) + a.shape[3:])
        return jnp.moveaxis(a, 2, 0)

    (C, n, m), h = lax.scan(mlstm_chunk, (C0, n0, m0),
                            (chunks(q), chunks(k), chunks(v), chunks(ig), chunks(lf)))
    h = jnp.moveaxis(h, 0, 2).reshape(B, H_M, T, D_HM)
    return jnp.swapaxes(h, 1, 2), C, n, m


def fox_prompt(q, k, v, lf):
    B, T = q.shape[:2]
    nb = T // Q_BLOCK
    F = jnp.swapaxes(jnp.cumsum(lf, axis=1), 1, 2)
    qb = jnp.moveaxis(q.reshape(B, nb, Q_BLOCK, H_F, D_HF), 1, 0)
    Fb = jnp.moveaxis(F.reshape(B, H_F, nb, Q_BLOCK), 2, 0)
    key_pos = jnp.arange(T)

    def block(args):
        i, qi, Fi = args
        q_pos = i * Q_BLOCK + jnp.arange(Q_BLOCK)
        logits = jnp.einsum('bqhd,bkhd->bhqk', qi, k) * FOX_SCALE + (Fi[..., :, None] - F[..., None, :])
        logits = jnp.where(key_pos[None, :] <= q_pos[:, None], logits, -jnp.inf)
        p = jax.nn.softmax(logits, axis=-1)
        return jnp.einsum('bhqk,bkhd->bqhd', p, v)

    out = lax.map(block, (jnp.arange(nb), qb, Fb))
    return jnp.moveaxis(out, 0, 1).reshape(B, T, H_F, D_HF)


def fox_sample(q, k, v, lf, cache_k, cache_v, cache_logf, page_table, layer):
    DB, DS = q.shape[:2]
    past = page_table.shape[1] * PAGE_SIZE
    kp = cache_k[layer, page_table].astype(jnp.float32).reshape(DB, past, H_F, D_HF)
    vp = cache_v[layer, page_table].astype(jnp.float32).reshape(DB, past, H_F, D_HF)
    lfp = cache_logf[layer, page_table].astype(jnp.float32).reshape(DB, past, H_F)
    F_past = jnp.swapaxes(jnp.cumsum(lfp, axis=1), 1, 2)
    F_new = F_past[..., -1:] + jnp.swapaxes(jnp.cumsum(lf, axis=1), 1, 2)
    logit_p = jnp.einsum('bqhd,bkhd->bhqk', q, kp) * FOX_SCALE + (F_new[..., :, None] - F_past[..., None, :])
    causal = jnp.tril(jnp.ones((DS, DS), dtype=bool))
    logit_n = jnp.where(causal,
                        jnp.einsum('bqhd,bkhd->bhqk', q, k) * FOX_SCALE + (F_new[..., :, None] - F_new[..., None, :]),
                        -jnp.inf)
    p = jax.nn.softmax(jnp.concatenate([logit_p, logit_n], axis=-1), axis=-1)
    return (jnp.einsum('bhqk,bkhd->bqhd', p[..., :past], vp)
            + jnp.einsum('bhqk,bkhd->bqhd', p[..., past:], v))


def merge_heads(h_m, o_m, h_f, g_out_m, g_out_f, w_out, dtype):
    B, T = h_m.shape[:2]
    ym = rms_norm(o_m * h_m, g_out_m).reshape(B, T, W_M)
    yf = rms_norm(h_f, g_out_f).reshape(B, T, W_F)
    y = jnp.concatenate([ym, yf], axis=-1).astype(dtype)
    return jnp.einsum('btw,wd->btd', y, w_out)


def hier_moe(h, w_group, b_group, w_router, b_router, w_gate, w_up, w_down):
    f32 = jnp.float32
    g_logits = (h @ w_group).astype(f32) + b_group.astype(f32)
    g_prob = jax.nn.softmax(g_logits, axis=-1)
    g_onehot = jax.nn.one_hot(jnp.argmax(g_logits, axis=-1), N_GROUPS, dtype=f32)
    g_w = jnp.sum(g_prob * g_onehot, axis=-1)
    e_logits = jnp.einsum('nd,gde->nge', h, w_router).astype(f32) + b_router.astype(f32)
    e_prob = jax.nn.softmax(jnp.einsum('nge,ng->ne', e_logits, g_onehot), axis=-1)
    top_v, top_i = lax.top_k(e_prob, TOP_K_IN_GROUP)
    top_v = top_v / jnp.sum(top_v, axis=-1, keepdims=True)
    e_w = jnp.einsum('nk,nke->ne', top_v, jax.nn.one_hot(top_i, EXP_PER_GROUP, dtype=f32))
    combine = g_onehot[:, :, None] * e_w[:, None, :] * g_w[:, None, None]
    a = jnp.einsum('nd,gedf->ngef', h, w_gate).astype(f32)
    u = jnp.einsum('nd,gedf->ngef', h, w_up).astype(f32)
    act = (jax.nn.silu(a) * u * combine[..., None]).astype(h.dtype)
    return jnp.einsum('ngef,gefd->nd', act, w_down)


def channel_sublayer(x, g, shift, scale, gate, w_group, b_group, w_router, b_router, w_gate, w_up, w_down):
    h = modulate(x, g, shift, scale)
    B, T, D = h.shape
    y = hier_moe(h.reshape(B * T, D), w_group, b_group, w_router, b_router, w_gate, w_up, w_down).reshape(B, T, D)
    return x + (gate * y).astype(x.dtype)


def setup_inputs(seed: int = 0) -> dict:
    key = jax.random.key(seed)
    ks = jax.random.split(key, 32)
    f32 = jnp.float32
    n_pages = PAST_LEN // PAGE_SIZE
    n_pool = (5 * DEC_BATCH * n_pages) // 4

    def nrm(k, shape, s):
        return jax.random.normal(k, shape, f32) * s

    inp = {}
    inp['x_prompt'] = nrm(ks[0], (BATCH, SEQ, D_MODEL), 1.0)
    inp['x_sample'] = nrm(ks[1], (DEC_BATCH, DEC_SEQ, D_MODEL), 1.0)
    inp['c_prompt'] = nrm(ks[2], (BATCH, D_MODEL), 1.0)
    inp['c_sample'] = nrm(ks[3], (DEC_BATCH, D_MODEL), 1.0)
    inp['cache_k'] = nrm(ks[4], (DEPTH, n_pool, PAGE_SIZE, H_F, D_HF), 1.0)
    inp['cache_v'] = nrm(ks[5], (DEPTH, n_pool, PAGE_SIZE, H_F, D_HF), 1.0)
    inp['cache_logf'] = jax.nn.log_sigmoid(nrm(ks[6], (DEPTH, n_pool, PAGE_SIZE, H_F), 1.0) + 3.0)
    inp['state_C'] = nrm(ks[7], (DEPTH, DEC_BATCH, H_M, D_HM, D_HM), 0.1)
    inp['state_n'] = nrm(ks[8], (DEPTH, DEC_BATCH, H_M, D_HM), 0.1)
    inp['state_m'] = nrm(ks[9], (DEPTH, DEC_BATCH, H_M), 1.0)
    inp['page_table'] = jax.random.permutation(ks[10], n_pool)[: DEC_BATCH * n_pages].reshape(
        DEC_BATCH, n_pages).astype(jnp.int32)
    inp['w_ada'] = nrm(ks[11], (DEPTH, D_MODEL, 6 * D_MODEL), 0.5 * D_MODEL ** -0.5)
    inp['b_ada'] = nrm(ks[12], (DEPTH, 6 * D_MODEL), 0.02)
    inp['g_norm1'] = 1.0 + nrm(ks[13], (DEPTH, D_MODEL), 0.02)
    inp['g_norm2'] = 1.0 + nrm(ks[14], (DEPTH, D_MODEL), 0.02)
    inp['w_in'] = nrm(ks[15], (DEPTH, D_MODEL, PROJ_WIDTH), D_MODEL ** -0.5)
    inp['b_ig'] = nrm(ks[16], (DEPTH, H_M), 0.1)
    inp['b_fg'] = jnp.linspace(3.0, 6.0, H_M)[None, :] + nrm(ks[17], (DEPTH, H_M), 0.1)
    inp['b_ff'] = jnp.linspace(1.0, 4.0, H_F)[None, :] + nrm(ks[18], (DEPTH, H_F), 0.1)
    inp['g_out_m'] = 1.0 + nrm(ks[19], (DEPTH, H_M, D_HM), 0.02)
    inp['g_out_f'] = 1.0 + nrm(ks[20], (DEPTH, H_F, D_HF), 0.02)
    inp['w_out'] = nrm(ks[21], (DEPTH, MIX_WIDTH, D_MODEL), MIX_WIDTH ** -0.5)
    inp['w_group'] = nrm(ks[22], (DEPTH, D_MODEL, N_GROUPS), D_MODEL ** -0.5)
    inp['b_group'] = nrm(ks[23], (DEPTH, N_GROUPS), 0.01)
    inp['w_router'] = nrm(ks[24], (DEPTH, N_GROUPS, D_MODEL, EXP_PER_GROUP), D_MODEL ** -0.5)
    inp['b_router'] = nrm(ks[25], (DEPTH, N_GROUPS, EXP_PER_GROUP), 0.01)
    inp['w_gate'] = nrm(ks[26], (DEPTH, N_GROUPS, EXP_PER_GROUP, D_MODEL, D_FF_E), D_MODEL ** -0.5)
    inp['w_up'] = nrm(ks[27], (DEPTH, N_GROUPS, EXP_PER_GROUP, D_MODEL, D_FF_E), D_MODEL ** -0.5)
    inp['w_down'] = nrm(ks[28], (DEPTH, N_GROUPS, EXP_PER_GROUP, D_FF_E, D_MODEL), D_FF_E ** -0.5)
    inp['g_final'] = 1.0 + nrm(ks[29], (D_MODEL,), 0.02)
    return inp


def reference(x_prompt, x_sample, c_prompt, c_sample, cache_k, cache_v, cache_logf, state_C, state_n, state_m,
              page_table, w_ada, b_ada, g_norm1, g_norm2, w_in, b_ig, b_fg, b_ff, g_out_m, g_out_f, w_out,
              w_group, b_group, w_router, b_router, w_gate, w_up, w_down, g_final):
    xp, xs = x_prompt, x_sample
    B = xp.shape[0]
    names = ['k_p', 'v_p', 'lf_p', 'C_p', 'n_p', 'm_p', 'k_s', 'v_s', 'lf_s', 'C_s', 'n_s', 'm_s']
    new = {nm: [] for nm in names}
    for l in range(DEPTH):
        sp1, sc1, gt1, sp2, sc2, gt2 = adaln_params(c_prompt, w_ada[l], b_ada[l])
        ss1, ssc1, sg1, ss2, ssc2, sg2 = adaln_params(c_sample, w_ada[l], b_ada[l])

        hp = modulate(xp, g_norm1[l], sp1, sc1)
        mq, mk, mv, mo, mi, mlf, fq, fk, fv, flf = mixer_inputs(hp, w_in[l], b_ig[l], b_fg[l], b_ff[l])
        C0 = jnp.zeros((B, H_M, D_HM, D_HM), jnp.float32)
        n0 = jnp.zeros((B, H_M, D_HM), jnp.float32)
        m0 = jnp.zeros((B, H_M), jnp.float32)
        hm, Cp, n_p, m_p = mlstm_sequence(mq, mk, mv, mi, mlf, C0, n0, m0, CHUNK)
        hf = fox_prompt(fq, fk, fv, flf)
        mix = merge_heads(hm, mo, hf, g_out_m[l], g_out_f[l], w_out[l], xp.dtype)
        xp = xp + (gt1 * mix).astype(xp.dtype)
        for nm, val in zip(names[:6], (fk, fv, flf, Cp, n_p, m_p)):
            new[nm].append(val)

        hs = modulate(xs, g_norm1[l], ss1, ssc1)
        sq, sk, sv, so, si, slf, tq, tk, tv, tlf = mixer_inputs(hs, w_in[l], b_ig[l], b_fg[l], b_ff[l])
        hm_s, Cs, n_s, m_s = mlstm_sequence(sq, sk, sv, si, slf, state_C[l].astype(jnp.float32),
                                            state_n[l].astype(jnp.float32), state_m[l].astype(jnp.float32),
                                            xs.shape[1])
        hf_s = fox_sample(tq, tk, tv, tlf, cache_k, cache_v, cache_logf, page_table, l)
        mix_s = merge_heads(hm_s, so, hf_s, g_out_m[l], g_out_f[l], w_out[l], xs.dtype)
        xs = xs + (sg1 * mix_s).astype(xs.dtype)
        for nm, val in zip(names[6:], (tk, tv, tlf, Cs, n_s, m_s)):
            new[nm].append(val)

        xp = channel_sublayer(xp, g_norm2[l], sp2, sc2, gt2, w_group[l], b_group[l], w_router[l], b_router[l],
                              w_gate[l], w_up[l], w_down[l])
        xs = channel_sublayer(xs, g_norm2[l], ss2, ssc2, sg2, w_group[l], b_group[l], w_router[l], b_router[l],
                              w_gate[l], w_up[l], w_down[l])

    y_prompt = rms_norm(xp, g_final)
    y_sample = rms_norm(xs, g_final)
    return (y_prompt, y_sample,
            jnp.stack(new['k_p']), jnp.stack(new['v_p']), jnp.stack(new['lf_p']),
            jnp.stack(new['C_p']), jnp.stack(new['n_p']), jnp.stack(new['m_p']),
            jnp.stack(new['k_s']), jnp.stack(new['v_s']), jnp.stack(new['lf_s']),
            jnp.stack(new['C_s']), jnp.stack(new['n_s']), jnp.stack(new['m_s']))
```

```python
import functools

import jax
import jax.numpy as jnp
import numpy as np
from jax import lax
from jax.experimental import pallas as pl
from jax.experimental.pallas import tpu as pltpu

F32 = jnp.float32
BF16 = jnp.bfloat16
HIGHEST = lax.Precision.HIGHEST

D_MODEL = 1024
H_M = 4
D_HM = 128
W_M = H_M * D_HM
H_F = 8
D_HF = 64
W_F = H_F * D_HF
PAGE_SIZE = 128
CHUNK = 128
DEC_SEQ = 4
N_GROUPS = 4
EXP_PER_GROUP = 4
N_EXPERTS = N_GROUPS * EXP_PER_GROUP
D_FF_E = D_MODEL // 4
FOX_SCALE = D_HF ** -0.5
EPS = 1e-6

LANES = 128
GATE_COLS = LANES
SAMPLE_ROWS = 16
VMEM_LIMIT = 56 * 1024 * 1024


def _params(*sem):
    return pltpu.CompilerParams(dimension_semantics=sem, vmem_limit_bytes=VMEM_LIMIT)


def _dot(a, b):
    return jnp.dot(a, b, preferred_element_type=F32)


def _dot_nt(a, b):
    return lax.dot_general(a, b, (((1,), (1,)), ((), ())), preferred_element_type=F32)


def _dot_tn(a, b):
    return lax.dot_general(a, b, (((0,), (0,)), ((), ())), preferred_element_type=F32)


def _dot_hi(a, b):
    return jnp.dot(a, b, preferred_element_type=F32, precision=HIGHEST)


def _rms(x, g):
    return x * lax.rsqrt(jnp.mean(x * x, axis=-1, keepdims=True) + EPS) * g


def _log_sigmoid(x):
    return jnp.minimum(x, 0.0) - jnp.log1p(jnp.exp(-jnp.abs(x)))


def _split_bf16(x):
    hi = x.astype(BF16)
    lo = (x - hi.astype(F32)).astype(BF16)
    return hi, lo


def _adaln_kernel(c_ref, w_ref, b_ref, o_ref):
    c = c_ref[...]
    o_ref[...] = _dot_hi(c * jax.nn.sigmoid(c), w_ref[...]) + b_ref[...]


def _adaln(c_all, w_ada, b_ada):
    rows = c_all.shape[0]
    n = w_ada.shape[1]
    tn = 1536
    return pl.pallas_call(
        _adaln_kernel,
        grid=(n // tn,),
        in_specs=[
            pl.BlockSpec((rows, D_MODEL), lambda j: (0, 0)),
            pl.BlockSpec((D_MODEL, tn), lambda j: (0, j)),
            pl.BlockSpec((1, tn), lambda j: (0, j)),
        ],
        out_specs=pl.BlockSpec((rows, tn), lambda j: (0, j)),
        out_shape=jax.ShapeDtypeStruct((rows, n), F32),
        compiler_params=_params("arbitrary"),
        name="adaln",
    )(c_all, w_ada, b_ada.reshape(1, n))


def _inproj_kernel(x_ref, shift_ref, scale_ref, g_ref, w_ref, wg_ref, gb_ref,
                   mq_ref, mk_ref, mv_ref, mo_ref, fq_ref, fk_ref, fv_ref, fk16_ref, fv16_ref, gate_ref):
    h = _rms(x_ref[...], g_ref[...]) * (1.0 + scale_ref[...]) + shift_ref[...]
    hb, h_lo = _split_bf16(h)

    def proj(j):
        return _dot(hb, w_ref[:, j * W_M:(j + 1) * W_M])

    mq_ref[...] = proj(0).astype(BF16)
    mk_ref[...] = (proj(1) * (D_HM ** -0.5)).astype(BF16)
    mv_ref[...] = proj(2).astype(BF16)
    mo_ref[...] = jax.nn.sigmoid(proj(3))
    fq_ref[...] = (proj(4) * FOX_SCALE).astype(BF16)
    fk = proj(5)
    fk_ref[...] = fk
    fk16_ref[...] = fk.astype(BF16)
    fv = proj(6)
    fv_ref[...] = fv
    fv16_ref[...] = fv.astype(BF16)
    gg = _dot(hb, wg_ref[...])
    gpre = gg[:, :GATE_COLS] + gg[:, GATE_COLS:] + _dot(h_lo, wg_ref[:, :GATE_COLS]) + gb_ref[...]
    lane = lax.broadcasted_iota(jnp.int32, gpre.shape, 1)
    gate_ref[...] = jnp.where(lane < H_M, gpre, jnp.where(lane < 2 * H_M + H_F, _log_sigmoid(gpre), 0.0))


def _inproj(x, shift, scale, g_norm, w_main, w_gate2, gate_bias, tm):
    t = x.shape[0]
    mod_rows = shift.shape[0]
    mod_map = (lambda i: (0, 0)) if mod_rows == 1 else (lambda i: (i, 0))
    mod_block = (1, D_MODEL) if mod_rows == 1 else (tm, D_MODEL)
    tok = lambda w: pl.BlockSpec((tm, w), lambda i: (i, 0))
    const = lambda shape: pl.BlockSpec(shape, lambda i: (0, 0))
    out_shapes = (
        jax.ShapeDtypeStruct((t, W_M), BF16), jax.ShapeDtypeStruct((t, W_M), BF16),
        jax.ShapeDtypeStruct((t, W_M), BF16), jax.ShapeDtypeStruct((t, W_M), F32),
        jax.ShapeDtypeStruct((t, W_F), BF16), jax.ShapeDtypeStruct((t, W_F), F32),
        jax.ShapeDtypeStruct((t, W_F), F32), jax.ShapeDtypeStruct((t, W_F), BF16),
        jax.ShapeDtypeStruct((t, W_F), BF16), jax.ShapeDtypeStruct((t, GATE_COLS), F32),
    )
    return pl.pallas_call(
        _inproj_kernel,
        grid=(t // tm,),
        in_specs=[
            tok(D_MODEL),
            pl.BlockSpec(mod_block, mod_map), pl.BlockSpec(mod_block, mod_map),
            const((1, D_MODEL)), const(w_main.shape), const(w_gate2.shape), const((1, GATE_COLS)),
        ],
        out_specs=[tok(W_M)] * 4 + [tok(W_F)] * 5 + [tok(GATE_COLS)],
        out_shape=out_shapes,
        compiler_params=_params("arbitrary"),
        name="inproj",
    )(x, shift, scale, g_norm, w_main, w_gate2, gate_bias)


def _mlstm_prompt_kernel(q_ref, k_ref, v_ref, o_ref, g_ref, gout_ref,
                         ym_ref, fcol_ref, frow_ref, c_ref, nm_ref, fcarry_ref):
    L = CHUNK

    @pl.when(pl.program_id(0) == 0)
    def _init():
        c_ref[...] = jnp.zeros_like(c_ref)
        nm_ref[...] = jnp.zeros_like(nm_ref)
        fcarry_ref[...] = jnp.zeros_like(fcarry_ref)

    row = lax.broadcasted_iota(jnp.int32, (L, L), 0)
    col = lax.broadcasted_iota(jnp.int32, (L, L), 1)
    causal = row >= col
    gates = g_ref[...]
    cum = _dot_hi(causal.astype(F32), gates)
    gates_t = gates.T
    cum_t = cum.T
    for h in range(H_M):
        sl = slice(h * D_HM, (h + 1) * D_HM)
        a_row = gates_t[h:h + 1, :] - cum_t[H_M + h:H_M + h + 1, :]
        b_col = cum[:, H_M + h:H_M + h + 1]
        a_col = gates[:, h:h + 1] - b_col
        m_prev = nm_ref[H_M + h:H_M + h + 1, 0:1]
        a_mat = jnp.where(causal, a_row, -jnp.inf)
        r = jnp.maximum(jnp.max(a_mat, axis=-1, keepdims=True), m_prev)
        w_intra = jnp.exp(a_mat - r)
        w_inter = jnp.exp(m_prev - r)
        q = q_ref[:, sl]
        k = k_ref[:, sl]
        v = v_ref[:, sl]
        s = _dot_nt(q, k) * w_intra
        c_old = c_ref[h]
        n_old = nm_ref[h:h + 1, :]
        num = _dot(s.astype(BF16), v) + w_inter * _dot_nt(q, c_old.astype(BF16))
        den = (jnp.sum(s, axis=-1, keepdims=True)
               + w_inter * jnp.sum(q.astype(F32) * n_old, axis=-1, keepdims=True))
        m_t = b_col + r
        hid = num / jnp.maximum(jnp.abs(den), jnp.exp(-m_t))
        m_new = m_t[L - 1:L, :]
        b_last = b_col[L - 1:L, :]
        w_state = jnp.exp(a_col + (b_last - m_new))
        decay = jnp.exp(b_last + m_prev - m_new)
        vw = (v.astype(F32) * w_state).astype(BF16)
        c_ref[h] = decay * c_old + _dot_tn(vw, k)
        nm_ref[h:h + 1, :] = decay * n_old + jnp.sum(k.astype(F32) * w_state, axis=0, keepdims=True)
        nm_ref[H_M + h:H_M + h + 1, :] = jnp.broadcast_to(m_new, (1, LANES))
        ym_ref[:, sl] = _rms(o_ref[:, sl] * hid, gout_ref[:, sl]).astype(BF16)
    f_glob = cum + fcarry_ref[...]
    fcol_ref[...] = f_glob
    frow_ref[...] = f_glob.T[2 * H_M:2 * H_M + H_F, :]
    fcarry_ref[...] = f_glob[L - 1:L, :]


def _mlstm_prompt(mq, mk, mv, mo, gates, g_out_m):
    t = mq.shape[0]
    tok = lambda w: pl.BlockSpec((CHUNK, w), lambda c: (c, 0))
    return pl.pallas_call(
        _mlstm_prompt_kernel,
        grid=(t // CHUNK,),
        in_specs=[tok(W_M), tok(W_M), tok(W_M), tok(W_M), tok(GATE_COLS),
                  pl.BlockSpec((1, W_M), lambda c: (0, 0))],
        out_specs=[
            tok(W_M), tok(GATE_COLS),
            pl.BlockSpec((H_F, CHUNK), lambda c: (0, c)),
            pl.BlockSpec((H_M, D_HM, D_HM), lambda c: (0, 0, 0)),
            pl.BlockSpec((2 * H_M, D_HM), lambda c: (0, 0)),
        ],
        out_shape=(
            jax.ShapeDtypeStruct((t, W_M), BF16), jax.ShapeDtypeStruct((t, GATE_COLS), F32),
            jax.ShapeDtypeStruct((H_F, t), F32),
            jax.ShapeDtypeStruct((H_M, D_HM, D_HM), F32), jax.ShapeDtypeStruct((2 * H_M, D_HM), F32),
        ),
        scratch_shapes=[pltpu.VMEM((1, GATE_COLS), F32)],
        compiler_params=_params("arbitrary"),
        name="mlstm_prompt",
    )(mq, mk, mv, mo, gates, g_out_m)


def _fox_prompt_kernel(qtab_ref, ktab_ref, q_ref, k_ref, v_ref, fcol_ref, frow_ref, gout_ref,
                       y_ref, m_scr, l_scr, acc_scr, *, tq):
    i = pl.program_id(0)
    qb = qtab_ref[i]
    kb = ktab_ref[i]

    @pl.when(kb == 0)
    def _init():
        m_scr[...] = jnp.full_like(m_scr, -jnp.inf)
        l_scr[...] = jnp.zeros_like(l_scr)
        acc_scr[...] = jnp.zeros_like(acc_scr)

    lo = lax.broadcasted_iota(jnp.int32, (1, LANES), 1) < D_HF

    def block(diagonal):
        if diagonal:
            keep = (lax.broadcasted_iota(jnp.int32, (tq, tq), 0) >= lax.broadcasted_iota(jnp.int32, (tq, tq), 1))
        for p in range(H_F // 2):
            sl = slice(p * LANES, (p + 1) * LANES)
            q2 = q_ref[:, sl]
            k2 = k_ref[:, sl]
            v2 = v_ref[:, sl]
            zero = jnp.zeros_like(q2)
            alphas, pvs = [], []
            for half, qm in enumerate((jnp.where(lo, q2, zero), jnp.where(lo, zero, q2))):
                h = 2 * p + half
                s = _dot_nt(qm, k2) + (fcol_ref[:, 2 * H_M + h:2 * H_M + h + 1] - frow_ref[h:h + 1, :])
                if diagonal:
                    s = jnp.where(keep, s, -jnp.inf)
                m_prev = m_scr[h]
                m_new = jnp.maximum(m_prev, jnp.max(s, axis=-1, keepdims=True))
                alpha = jnp.exp(m_prev - m_new)
                prob = jnp.exp(s - m_new)
                l_scr[h] = alpha * l_scr[h] + jnp.sum(prob, axis=-1, keepdims=True)
                m_scr[h] = m_new
                alphas.append(alpha)
                pvs.append(_dot(prob.astype(BF16), v2))
            acc = jnp.where(lo, alphas[0], alphas[1]) * acc_scr[:, sl] + jnp.where(lo, pvs[0], pvs[1])
            if diagonal:
                out = acc / jnp.where(lo, l_scr[2 * p], l_scr[2 * p + 1])
                sq = out * out
                ms = jnp.where(lo, jnp.sum(jnp.where(lo, sq, 0.0), axis=-1, keepdims=True),
                               jnp.sum(jnp.where(lo, 0.0, sq), axis=-1, keepdims=True)) * (1.0 / D_HF)
                y_ref[:, sl] = (out * lax.rsqrt(ms + EPS) * gout_ref[:, sl]).astype(BF16)
            else:
                acc_scr[:, sl] = acc

    @pl.when(kb < qb)
    def _off_diagonal():
        block(False)

    @pl.when(kb == qb)
    def _diagonal():
        block(True)


def _fox_prompt(fq, fk16, fv16, fcol, frow, g_out_f, tq):
    t = fq.shape[0]
    nb = t // tq
    pairs = [(q, k) for q in range(nb) for k in range(q + 1)]
    qtab = jnp.asarray(np.array([p[0] for p in pairs], np.int32))
    ktab = jnp.asarray(np.array([p[1] for p in pairs], np.int32))
    grid_spec = pltpu.PrefetchScalarGridSpec(
        num_scalar_prefetch=2,
        grid=(len(pairs),),
        in_specs=[
            pl.BlockSpec((tq, W_F), lambda i, qt, kt: (qt[i], 0)),
            pl.BlockSpec((tq, W_F), lambda i, qt, kt: (kt[i], 0)),
            pl.BlockSpec((tq, W_F), lambda i, qt, kt: (kt[i], 0)),
            pl.BlockSpec((tq, GATE_COLS), lambda i, qt, kt: (qt[i], 0)),
            pl.BlockSpec((H_F, tq), lambda i, qt, kt: (0, kt[i])),
            pl.BlockSpec((1, W_F), lambda i, qt, kt: (0, 0)),
        ],
        out_specs=pl.BlockSpec((tq, W_F), lambda i, qt, kt: (qt[i], 0)),
        scratch_shapes=[pltpu.VMEM((H_F, tq, 1), F32), pltpu.VMEM((H_F, tq, 1), F32), pltpu.VMEM((tq, W_F), F32)],
    )
    return pl.pallas_call(
        functools.partial(_fox_prompt_kernel, tq=tq),
        grid_spec=grid_spec,
        out_shape=jax.ShapeDtypeStruct((t, W_F), BF16),
        compiler_params=_params("arbitrary"),
        name="fox_prompt",
    )(qtab, ktab, fq, fk16, fv16, fcol, frow, g_out_f)


def _mlstm_sample_kernel(q_ref, k_ref, v_ref, o_ref, g_ref, gout_ref, c0_ref, n0_ref, m0_ref,
                         ym_ref, c_ref, n_ref, m_ref):
    R = SAMPLE_ROWS
    gates = g_ref[0]
    rows = [gates[0:1, :]]
    for t in range(1, DEC_SEQ):
        rows.append(rows[-1] + gates[t:t + 1, :])
    rows += [rows[-1]] * (R - DEC_SEQ)
    cum = jnp.concatenate(rows, axis=0)
    ridx = lax.broadcasted_iota(jnp.int32, (R, 1), 0)
    for h in range(H_M):
        sl = slice(h * D_HM, (h + 1) * D_HM)
        b_col = cum[:, H_M + h:H_M + h + 1]
        a_col = gates[:, h:h + 1] - b_col
        m_prev = m0_ref[0, h:h + 1, 0:1]
        cm = jnp.full((R, 1), -jnp.inf, F32)
        for s in range(DEC_SEQ):
            cm = jnp.maximum(cm, jnp.where(ridx >= s, a_col[s:s + 1, :], -jnp.inf))
        r = jnp.maximum(cm, m_prev)
        w_inter = jnp.exp(m_prev - r)
        q = q_ref[0, :, sl]
        k = k_ref[0, :, sl]
        v = v_ref[0, :, sl]
        q32, k32, v32 = q.astype(F32), k.astype(F32), v.astype(F32)
        c_old = c0_ref[0, h]
        n_old = n0_ref[0, h:h + 1, :]
        num = w_inter * _dot_nt(q, c_old.astype(BF16))
        den = w_inter * jnp.sum(q32 * n_old, axis=-1, keepdims=True)
        for s in range(DEC_SEQ):
            w_s = jnp.where(ridx >= s, jnp.exp(a_col[s:s + 1, :] - r), 0.0)
            s_col = jnp.sum(q32 * k32[s:s + 1, :], axis=-1, keepdims=True) * w_s
            num = num + s_col * v32[s:s + 1, :]
            den = den + s_col
        m_t = b_col + r
        hid = num / jnp.maximum(jnp.abs(den), jnp.exp(-m_t))
        m_new = m_t[DEC_SEQ - 1:DEC_SEQ, :]
        b_last = b_col[DEC_SEQ - 1:DEC_SEQ, :]
        w_state = jnp.where(ridx < DEC_SEQ, jnp.exp(a_col + (b_last - m_new)), 0.0)
        decay = jnp.exp(b_last + m_prev - m_new)
        pad = jnp.zeros((D_HM - R, D_HM), BF16)
        vw = jnp.concatenate([(v32 * w_state).astype(BF16), pad], axis=0)
        kp = jnp.concatenate([k, pad], axis=0)
        c_ref[0, h] = decay * c_old + _dot_tn(vw, kp)
        n_ref[0, h:h + 1, :] = decay * n_old + jnp.sum(k32 * w_state, axis=0, keepdims=True)
        m_ref[0, h:h + 1, :] = jnp.broadcast_to(m_new, (1, LANES))
        ym_ref[0, :, sl] = _rms(o_ref[0, :, sl] * hid, gout_ref[:, sl]).astype(BF16)


def _mlstm_sample(mq, mk, mv, mo, gates, g_out_m, c0, n0, m0):
    nb = mq.shape[0]
    seq = lambda w: pl.BlockSpec((1, SAMPLE_ROWS, w), lambda b: (b, 0, 0))
    st_c = pl.BlockSpec((1, H_M, D_HM, D_HM), lambda b: (b, 0, 0, 0))
    st_v = pl.BlockSpec((1, H_M, D_HM), lambda b: (b, 0, 0))
    return pl.pallas_call(
        _mlstm_sample_kernel,
        grid=(nb,),
        in_specs=[seq(W_M), seq(W_M), seq(W_M), seq(W_M), seq(GATE_COLS),
                  pl.BlockSpec((1, W_M), lambda b: (0, 0)), st_c, st_v, st_v],
        out_specs=[seq(W_M), st_c, st_v, st_v],
        out_shape=(
            jax.ShapeDtypeStruct((nb, SAMPLE_ROWS, W_M), BF16),
            jax.ShapeDtypeStruct((nb, H_M, D_HM, D_HM), F32),
            jax.ShapeDtypeStruct((nb, H_M, D_HM), F32), jax.ShapeDtypeStruct((nb, H_M, D_HM), F32),
        ),
        compiler_params=_params("arbitrary"),
        name="mlstm_sample",
    )(mq, mk, mv, mo, gates, g_out_m, c0, n0, m0)


def _fox_sample_kernel(pt_ref, q_ref, knew_ref, vnew_ref, lfnew_ref, gout_ref, *rest, pages, n_steps):
    k_refs = rest[:pages]
    v_refs = rest[pages:2 * pages]
    lf_refs = rest[2 * pages:3 * pages]
    y_ref, m_scr, l_scr, acc_scr, carry_scr, qbd_scr = rest[3 * pages:]
    step = pl.program_id(1)
    nq = DEC_SEQ * H_F

    @pl.when(step == 0)
    def _init():
        m_scr[...] = jnp.full_like(m_scr, -jnp.inf)
        l_scr[...] = jnp.zeros_like(l_scr)
        acc_scr[...] = jnp.zeros_like(acc_scr)
        carry_scr[...] = jnp.zeros_like(carry_scr)
        head_of_lane = lax.broadcasted_iota(jnp.int32, (H_F, W_F), 1) // D_HF
        own = head_of_lane == lax.broadcasted_iota(jnp.int32, (H_F, W_F), 0)
        q32 = q_ref[0].astype(F32)
        tiles = [jnp.where(own, jnp.broadcast_to(q32[i:i + 1, :], (H_F, W_F)), 0.0) for i in range(DEC_SEQ)]
        qbd_scr[...] = jnp.concatenate(tiles, axis=0).astype(BF16)

    def absorb(s, v):
        m_prev = m_scr[...]
        m_new = jnp.maximum(m_prev, jnp.max(s, axis=-1, keepdims=True))
        alpha = jnp.exp(m_prev - m_new)
        prob = jnp.exp(s - m_new)
        l_scr[...] = alpha * l_scr[...] + jnp.sum(prob, axis=-1, keepdims=True)
        acc_scr[...] = alpha * acc_scr[...] + _dot(prob.astype(BF16), v)
        m_scr[...] = m_new

    upper = (lax.broadcasted_iota(jnp.int32, (PAGE_SIZE, PAGE_SIZE), 0)
             <= lax.broadcasted_iota(jnp.int32, (PAGE_SIZE, PAGE_SIZE), 1)).astype(F32)
    qbd = qbd_scr[...]
    for j in range(pages):
        f_past = carry_scr[...] + _dot_hi(lf_refs[j][0], upper)
        carry_scr[...] = f_past[:, PAGE_SIZE - 1:PAGE_SIZE]
        s = _dot_nt(qbd, k_refs[j][0].astype(BF16)) - jnp.concatenate([f_past] * DEC_SEQ, axis=0)
        absorb(s, v_refs[j][0].astype(BF16))

    @pl.when(step == n_steps - 1)
    def _finish():
        r = SAMPLE_ROWS
        cn = _dot_hi(lfnew_ref[0], upper)[:, :r]
        bias = jnp.concatenate([cn + carry_scr[...]] * DEC_SEQ, axis=0)
        qi = lax.broadcasted_iota(jnp.int32, (nq, r), 0) // H_F
        kj = lax.broadcasted_iota(jnp.int32, (nq, r), 1)
        s = jnp.where(kj <= qi, _dot_nt(qbd, knew_ref[0]) - bias, -jnp.inf)
        absorb(s, vnew_ref[0])
        res = acc_scr[...] / l_scr[...]
        own = (lax.broadcasted_iota(jnp.int32, (nq, W_F), 1) // D_HF
               == lax.broadcasted_iota(jnp.int32, (nq, W_F), 0) % H_F)
        ms = jnp.sum(jnp.where(own, res * res, 0.0), axis=-1, keepdims=True) * (1.0 / D_HF)
        y = jnp.where(own, res * lax.rsqrt(ms + EPS) * gout_ref[...], 0.0)
        toks = [jnp.sum(y[H_F * i:H_F * (i + 1), :], axis=0, keepdims=True) for i in range(DEC_SEQ)]
        toks.append(jnp.zeros((r - DEC_SEQ, W_F), F32))
        y_ref[0] = jnp.concatenate(toks, axis=0).astype(BF16)


def _fox_sample(page_table, fq, fk16, fv16, lf_new_t, g_out_f, cache_k, cache_v, cache_lf_t, pages):
    nb, n_pages = page_table.shape
    n_steps = n_pages // pages
    seq = lambda w: pl.BlockSpec((1, SAMPLE_ROWS, w), lambda b, s, pt: (b, 0, 0))

    def paged(shape, j):
        return pl.BlockSpec((1,) + shape, lambda b, s, pt: (pt[b * n_pages + s * pages + j], 0, 0))

    nq = DEC_SEQ * H_F
    grid_spec = pltpu.PrefetchScalarGridSpec(
        num_scalar_prefetch=1,
        grid=(nb, n_steps),
        in_specs=([seq(W_F), seq(W_F), seq(W_F),
                   pl.BlockSpec((1, H_F, PAGE_SIZE), lambda b, s, pt: (b, 0, 0)),
                   pl.BlockSpec((1, W_F), lambda b, s, pt: (0, 0))]
                  + [paged((PAGE_SIZE, W_F), j) for j in range(pages)]
                  + [paged((PAGE_SIZE, W_F), j) for j in range(pages)]
                  + [paged((H_F, PAGE_SIZE), j) for j in range(pages)]),
        out_specs=seq(W_F),
        scratch_shapes=[pltpu.VMEM((nq, 1), F32), pltpu.VMEM((nq, 1), F32), pltpu.VMEM((nq, W_F), F32),
                        pltpu.VMEM((H_F, 1), F32), pltpu.VMEM((nq, W_F), BF16)],
    )
    return pl.pallas_call(
        functools.partial(_fox_sample_kernel, pages=pages, n_steps=n_steps),
        grid_spec=grid_spec,
        out_shape=jax.ShapeDtypeStruct((nb, SAMPLE_ROWS, W_F), BF16),
        compiler_params=_params("arbitrary", "arbitrary"),
        name="fox_sample",
    )(page_table.reshape(-1), fq, fk16, fv16, lf_new_t, g_out_f,
      *([cache_k] * pages), *([cache_v] * pages), *([cache_lf_t] * pages))


def _post_kernel(x_ref, ym_ref, yf_ref, gt1_ref, sh2_ref, sc2_ref, gt2_ref, g2_ref, gfin_ref,
                 wom_ref, wof_ref, wr_ref, br_ref, wg_ref, wu_ref, wd_ref,
                 o_ref, xp_scr, hb_scr, comb_scr, acc_scr):
    e = pl.program_id(1)

    @pl.when(e == 0)
    def _prepare():
        mix = _dot(ym_ref[...], wom_ref[...]) + _dot(yf_ref[...], wof_ref[...])
        xp = x_ref[...] + gt1_ref[...] * mix
        xp_scr[...] = xp
        h = _rms(xp, g2_ref[...]) * (1.0 + sc2_ref[...]) + sh2_ref[...]
        hb, h_lo = _split_bf16(h)
        hb_scr[...] = hb
        rr = _dot(hb, wr_ref[...])
        logits = rr[:, :LANES] + rr[:, LANES:] + _dot(h_lo, wr_ref[:, :LANES]) + br_ref[...]
        lane = lax.broadcasted_iota(jnp.int32, logits.shape, 1)
        big = jnp.int32(4 * LANES)
        gl = jnp.where(lane < N_GROUPS, logits, -jnp.inf)
        gmax = jnp.max(gl, axis=-1, keepdims=True)
        gidx = jnp.min(jnp.where(gl == gmax, lane, big), axis=-1, keepdims=True)
        g_w = 1.0 / jnp.sum(jnp.exp(gl - gmax), axis=-1, keepdims=True)
        first = N_GROUPS + EXP_PER_GROUP * gidx
        el = jnp.where((lane >= first) & (lane < first + EXP_PER_GROUP), logits, -jnp.inf)
        emax = jnp.max(el, axis=-1, keepdims=True)
        pe = jnp.exp(el - emax)
        prob = pe / jnp.sum(pe, axis=-1, keepdims=True)
        v1 = jnp.max(prob, axis=-1, keepdims=True)
        i1 = jnp.min(jnp.where(prob == v1, lane, big), axis=-1, keepdims=True)
        rest = jnp.where((lane == i1) | (el == -jnp.inf), -1.0, prob)
        v2 = jnp.max(rest, axis=-1, keepdims=True)
        i2 = jnp.min(jnp.where(rest == v2, lane, big), axis=-1, keepdims=True)
        tot = v1 + v2
        comb_scr[...] = jnp.where(lane == i1, v1 / tot, jnp.where(lane == i2, v2 / tot, 0.0)) * g_w
        acc_scr[...] = jnp.zeros_like(acc_scr)

    hb = hb_scr[...]
    a = _dot(hb, wg_ref[0])
    u = _dot(hb, wu_ref[0])
    lane = lax.broadcasted_iota(jnp.int32, comb_scr.shape, 1)
    ce = jnp.sum(jnp.where(lane == e + N_GROUPS, comb_scr[...], 0.0), axis=-1, keepdims=True)
    act = (a * jax.nn.sigmoid(a) * u * ce).astype(BF16)
    acc_scr[...] += _dot(act, wd_ref[0])

    @pl.when(e == N_EXPERTS - 1)
    def _finish():
        o_ref[...] = _rms(xp_scr[...] + gt2_ref[...] * acc_scr[...], gfin_ref[...])


def _post(x, ym, yf, gt1, sh2, sc2, gt2, g_norm2, g_final, wo_m, wo_f, w_route2, b_route, wg, wu, wd, tm):
    t = x.shape[0]
    mod_rows = gt1.shape[0]
    mod_map = (lambda i, e: (0, 0)) if mod_rows == 1 else (lambda i, e: (i, 0))
    mod = pl.BlockSpec((1, D_MODEL) if mod_rows == 1 else (tm, D_MODEL), mod_map)
    tok = lambda w: pl.BlockSpec((tm, w), lambda i, e: (i, 0))
    const = lambda shape: pl.BlockSpec(shape, lambda i, e: (0,) * len(shape))
    return pl.pallas_call(
        _post_kernel,
        grid=(t // tm, N_EXPERTS),
        in_specs=[
            tok(D_MODEL), tok(W_M), tok(W_F), mod, mod, mod, mod,
            const((1, D_MODEL)), const((1, D_MODEL)),
            const(wo_m.shape), const(wo_f.shape), const(w_route2.shape), const((1, LANES)),
            pl.BlockSpec((1, D_MODEL, D_FF_E), lambda i, e: (e, 0, 0)),
            pl.BlockSpec((1, D_MODEL, D_FF_E), lambda i, e: (e, 0, 0)),
            pl.BlockSpec((1, D_FF_E, D_MODEL), lambda i, e: (e, 0, 0)),
        ],
        out_specs=tok(D_MODEL),
        out_shape=jax.ShapeDtypeStruct((t, D_MODEL), F32),
        scratch_shapes=[pltpu.VMEM((tm, D_MODEL), F32), pltpu.VMEM((tm, D_MODEL), BF16),
                        pltpu.VMEM((tm, LANES), F32), pltpu.VMEM((tm, D_MODEL), F32)],
        compiler_params=_params("arbitrary", "arbitrary"),
        name="post",
    )(x, ym, yf, gt1, sh2, sc2, gt2, g_norm2, g_final, wo_m, wo_f, w_route2, b_route, wg, wu, wd)


def _pad_cols(w, n):
    return jnp.pad(w, ((0, 0), (0, n - w.shape[1])))


def _hi_lo_cols(w):
    hi, lo = _split_bf16(w)
    return jnp.concatenate([hi, lo], axis=1)


def _pad_seq(a):
    nb = a.shape[0] // DEC_SEQ
    return jnp.pad(a.reshape(nb, DEC_SEQ, a.shape[1]), ((0, 0), (0, SAMPLE_ROWS - DEC_SEQ), (0, 0)))


def _unpad_seq(a):
    return a[:, :DEC_SEQ].reshape(a.shape[0] * DEC_SEQ, a.shape[2])


def kernel(x_prompt, x_sample, c_prompt, c_sample, cache_k, cache_v, cache_logf, state_C, state_n, state_m,
           page_table, w_ada, b_ada, g_norm1, g_norm2, w_in, b_ig, b_fg, b_ff, g_out_m, g_out_f, w_out,
           w_group, b_group, w_router, b_router, w_gate, w_up, w_down, g_final):
    depth = w_ada.shape[0]
    assert depth == 1 and x_prompt.shape[0] == 1
    t = x_prompt.shape[1]
    nb, ds = x_sample.shape[:2]
    assert ds == DEC_SEQ
    ts = nb * ds
    xp = x_prompt.reshape(t, D_MODEL)
    xs = x_sample.reshape(ts, D_MODEL)
    l = 0

    w = w_in[l]
    o = np.cumsum([0, W_M, W_M, W_M, W_M, H_M, H_M, W_F, W_F, W_F, H_F])
    w_main = jnp.concatenate([w[:, o[0]:o[4]], w[:, o[6]:o[9]]], axis=1).astype(BF16)
    w_gates = _pad_cols(jnp.concatenate([w[:, o[4]:o[6]], w[:, o[9]:o[10]]], axis=1), GATE_COLS)
    w_gate2 = _hi_lo_cols(w_gates)
    gate_bias = _pad_cols(jnp.concatenate([b_ig[l], b_fg[l], b_ff[l]])[None, :], GATE_COLS)
    gm = g_out_m[l].reshape(1, W_M)
    gf = g_out_f[l].reshape(1, W_F)
    wo_m = w_out[l][:W_M].astype(BF16)
    wo_f = w_out[l][W_M:].astype(BF16)
    w_route = jnp.concatenate(
        [w_group[l], jnp.transpose(w_router[l], (1, 0, 2)).reshape(D_MODEL, N_EXPERTS)], axis=1)
    w_route2 = _hi_lo_cols(_pad_cols(w_route, LANES))
    b_route = _pad_cols(jnp.concatenate([b_group[l], b_router[l].reshape(-1)])[None, :], LANES)
    wg = w_gate[l].reshape(N_EXPERTS, D_MODEL, D_FF_E).astype(BF16)
    wu = w_up[l].reshape(N_EXPERTS, D_MODEL, D_FF_E).astype(BF16)
    wd = w_down[l].reshape(N_EXPERTS, D_FF_E, D_MODEL).astype(BF16)

    rows = -(-(1 + nb) // 8) * 8
    c_all = jnp.pad(jnp.concatenate([c_prompt, c_sample], axis=0), ((0, rows - 1 - nb), (0, 0)))
    mod = _adaln(c_all, w_ada[l], b_ada[l])
    p_mod = [mod[0:1, i * D_MODEL:(i + 1) * D_MODEL] for i in range(6)]
    s_mod = [jnp.repeat(mod[1:1 + nb, i * D_MODEL:(i + 1) * D_MODEL], ds, axis=0) for i in range(6)]
    g1 = g_norm1[l][None, :]
    g2 = g_norm2[l][None, :]
    gfin = g_final[None, :]

    tm = min(512, t)
    mq, mk, mv, mo, fq, fk, fv, fk16, fv16, gates = _inproj(xp, p_mod[0], p_mod[1], g1, w_main, w_gate2, gate_bias, tm)
    ym, fcol, frow, c_p, nm_p = _mlstm_prompt(mq, mk, mv, mo, gates, gm)
    yf = _fox_prompt(fq, fk16, fv16, fcol, frow, gf, tm)
    y_prompt = _post(xp, ym, yf, p_mod[2], p_mod[3], p_mod[4], p_mod[5], g2, gfin,
                     wo_m, wo_f, w_route2, b_route, wg, wu, wd, tm)

    smq, smk, smv, smo, sfq, sfk, sfv, sfk16, sfv16, sgates = _inproj(
        xs, s_mod[0], s_mod[1], g1, w_main, w_gate2, gate_bias, ts)
    m0 = jnp.broadcast_to(state_m[l][:, :, None], (nb, H_M, LANES))
    sym, c_s, n_s, m_s = _mlstm_sample(_pad_seq(smq), _pad_seq(smk), _pad_seq(smv), _pad_seq(smo), _pad_seq(sgates),
                                       gm, state_C[l], state_n[l], m0)
    lf_new = sgates[:, 2 * H_M:2 * H_M + H_F]
    lf_new_t = jnp.pad(jnp.transpose(lf_new.reshape(nb, ds, H_F), (0, 2, 1)), ((0, 0), (0, 0), (0, PAGE_SIZE - ds)))
    n_pool = cache_k.shape[1]
    syf = _fox_sample(page_table, _pad_seq(sfq), _pad_seq(sfk16), _pad_seq(sfv16), lf_new_t, gf,
                      cache_k[l].reshape(n_pool, PAGE_SIZE, W_F), cache_v[l].reshape(n_pool, PAGE_SIZE, W_F),
                      jnp.transpose(cache_logf[l], (0, 2, 1)), pages=8)
    y_sample = _post(xs, _unpad_seq(sym), _unpad_seq(syf), s_mod[2], s_mod[3], s_mod[4], s_mod[5], g2, gfin,
                     wo_m, wo_f, w_route2, b_route, wg, wu, wd, ts)

    return (
        y_prompt.reshape(1, t, D_MODEL), y_sample.reshape(nb, ds, D_MODEL),
        fk.reshape(1, 1, t, H_F, D_HF), fv.reshape(1, 1, t, H_F, D_HF),
        gates[:, 2 * H_M:2 * H_M + H_F].reshape(1, 1, t, H_F),
        c_p[None, None], nm_p[None, None, :H_M], nm_p[None, None, H_M:, 0],
        sfk.reshape(1, nb, ds, H_F, D_HF), sfv.reshape(1, nb, ds, H_F, D_HF), lf_new.reshape(1, nb, ds, H_F),
        c_s[None], n_s[None], m_s[None, :, :, 0],
    )
```

```python
import functools
import math

import jax
import jax.numpy as jnp
import numpy as np
from jax import lax
from jax.experimental import pallas as pl
from jax.experimental.pallas import tpu as pltpu

F32 = jnp.float32
BF16 = jnp.bfloat16
HIGHEST = lax.Precision.HIGHEST

D_MODEL = 1024
H_M = 4
D_HM = 128
W_M = H_M * D_HM
H_F = 8
D_HF = 64
W_F = H_F * D_HF
PAGE_SIZE = 128
CHUNK = 128
DEC_SEQ = 4
N_GROUPS = 4
EXP_PER_GROUP = 4
N_EXPERTS = N_GROUPS * EXP_PER_GROUP
D_FF_E = D_MODEL // 4
FOX_SCALE = D_HF ** -0.5
LOG2E = math.log2(math.e)
EPS = 1e-6

LANES = 128
BF16_SUBLANES = 16
MXU_DIM = 256
GATE_COLS = LANES
FOX_COL = 2 * H_M
SAMPLE_ROWS = BF16_SUBLANES
AUGW = MXU_DIM
VROWS = D_HF + BF16_SUBLANES
FOX_SAMPLE_PAGES = 16
VMEM_LIMIT = 56 * 1024 * 1024


def _params(*sem):
    return pltpu.CompilerParams(dimension_semantics=sem, vmem_limit_bytes=VMEM_LIMIT)


def _dot(a, b):
    return jnp.dot(a, b, preferred_element_type=F32)


def _dot_nt(a, b):
    return lax.dot_general(a, b, (((1,), (1,)), ((), ())), preferred_element_type=F32)


def _dot_tn(a, b):
    return lax.dot_general(a, b, (((0,), (0,)), ((), ())), preferred_element_type=F32)


def _dot_hi(a, b):
    return jnp.dot(a, b, preferred_element_type=F32, precision=HIGHEST)


def _rms(x, g):
    return x * lax.rsqrt(jnp.mean(x * x, axis=-1, keepdims=True) + EPS) * g


def _log_sigmoid(x):
    return jnp.minimum(x, 0.0) - jnp.log1p(jnp.exp(-jnp.abs(x)))


def _split_bf16(x):
    hi = x.astype(BF16)
    lo = (x - hi.astype(F32)).astype(BF16)
    return hi, lo


def _split3(x):
    hi = x.astype(BF16).astype(F32)
    mid = (x - hi).astype(BF16).astype(F32)
    lo = (x - hi - mid).astype(BF16).astype(F32)
    return hi, mid, lo


def _adaln_kernel(c_ref, w_ref, b_ref, o_ref):
    c = c_ref[...]
    o_ref[...] = _dot_hi(c * jax.nn.sigmoid(c), w_ref[...]) + b_ref[...]


def _adaln(c_all, w_ada, b_ada):
    rows = c_all.shape[0]
    n = w_ada.shape[1]
    tn = 1536
    return pl.pallas_call(
        _adaln_kernel,
        grid=(n // tn,),
        in_specs=[
            pl.BlockSpec((rows, D_MODEL), lambda j: (0, 0)),
            pl.BlockSpec((D_MODEL, tn), lambda j: (0, j)),
            pl.BlockSpec((1, tn), lambda j: (0, j)),
        ],
        out_specs=pl.BlockSpec((rows, tn), lambda j: (0, j)),
        out_shape=jax.ShapeDtypeStruct((rows, n), F32),
        compiler_params=_params("arbitrary"),
        name="adaln",
    )(c_all, w_ada, b_ada.reshape(1, n))


def _inproj_kernel(x_ref, shift_ref, scale_ref, g_ref, w_ref, wg_ref, gb_ref,
                   mq_ref, mk_ref, mv_ref, mo_ref, fq_ref, fk_ref, fv_ref, fk16_ref, fv16_ref, gate_ref):
    h = _rms(x_ref[...], g_ref[...]) * (1.0 + scale_ref[...]) + shift_ref[...]
    hb, h_lo = _split_bf16(h)

    def proj(j):
        return _dot(hb, w_ref[:, j * W_M:(j + 1) * W_M])

    mq_ref[...] = proj(0).astype(BF16)
    mk_ref[...] = (proj(1) * (D_HM ** -0.5)).astype(BF16)
    mv_ref[...] = proj(2).astype(BF16)
    mo_ref[...] = jax.nn.sigmoid(proj(3))
    fq_ref[...] = (proj(4) * (FOX_SCALE * LOG2E)).astype(BF16)
    fk = proj(5)
    fk_ref[...] = fk
    fk16_ref[...] = fk.astype(BF16)
    fv = proj(6)
    fv_ref[...] = fv
    fv16_ref[...] = fv.astype(BF16)
    gg = _dot(hb, wg_ref[...])
    gpre = gg[:, :GATE_COLS] + gg[:, GATE_COLS:] + _dot(h_lo, wg_ref[:, :GATE_COLS]) + gb_ref[...]
    lane = lax.broadcasted_iota(jnp.int32, gpre.shape, 1)
    gate_ref[...] = jnp.where(lane < H_M, gpre, jnp.where(lane < FOX_COL + H_F, _log_sigmoid(gpre), 0.0))


def _inproj(x, shift, scale, g_norm, w_main, w_gate2, gate_bias, tm):
    t = x.shape[0]
    mod_rows = shift.shape[0]
    mod_map = (lambda i: (0, 0)) if mod_rows == 1 else (lambda i: (i, 0))
    mod_block = (1, D_MODEL) if mod_rows == 1 else (tm, D_MODEL)
    tok = lambda w: pl.BlockSpec((tm, w), lambda i: (i, 0))
    const = lambda shape: pl.BlockSpec(shape, lambda i: (0, 0))
    out_shapes = (
        jax.ShapeDtypeStruct((t, W_M), BF16), jax.ShapeDtypeStruct((t, W_M), BF16),
        jax.ShapeDtypeStruct((t, W_M), BF16), jax.ShapeDtypeStruct((t, W_M), F32),
        jax.ShapeDtypeStruct((t, W_F), BF16), jax.ShapeDtypeStruct((t, W_F), F32),
        jax.ShapeDtypeStruct((t, W_F), F32), jax.ShapeDtypeStruct((t, W_F), BF16),
        jax.ShapeDtypeStruct((t, W_F), BF16), jax.ShapeDtypeStruct((t, GATE_COLS), F32),
    )
    return pl.pallas_call(
        _inproj_kernel,
        grid=(t // tm,),
        in_specs=[
            tok(D_MODEL),
            pl.BlockSpec(mod_block, mod_map), pl.BlockSpec(mod_block, mod_map),
            const((1, D_MODEL)), const(w_main.shape), const(w_gate2.shape), const((1, GATE_COLS)),
        ],
        out_specs=[tok(W_M)] * 4 + [tok(W_F)] * 5 + [tok(GATE_COLS)],
        out_shape=out_shapes,
        compiler_params=_params("arbitrary"),
        name="inproj",
    )(x, shift, scale, g_norm, w_main, w_gate2, gate_bias)


def _mlstm_prompt_kernel(q_ref, k_ref, v_ref, o_ref, g_ref, gout_ref,
                         ym_ref, fcol_ref, c_ref, nm_ref, fcarry_ref):
    L = CHUNK

    @pl.when(pl.program_id(0) == 0)
    def _init():
        c_ref[...] = jnp.zeros_like(c_ref)
        nm_ref[...] = jnp.zeros_like(nm_ref)
        fcarry_ref[...] = jnp.zeros_like(fcarry_ref)

    row = lax.broadcasted_iota(jnp.int32, (L, L), 0)
    col = lax.broadcasted_iota(jnp.int32, (L, L), 1)
    causal = row >= col
    gates = g_ref[...]
    cum = _dot_hi(causal.astype(F32), gates)
    gates_t = gates.T
    cum_t = cum.T
    for h in range(H_M):
        sl = slice(h * D_HM, (h + 1) * D_HM)
        a_row = gates_t[h:h + 1, :] - cum_t[H_M + h:H_M + h + 1, :]
        b_col = cum[:, H_M + h:H_M + h + 1]
        a_col = gates[:, h:h + 1] - b_col
        m_prev = nm_ref[H_M + h:H_M + h + 1, 0:1]
        a_mat = jnp.where(causal, a_row, -jnp.inf)
        r = jnp.maximum(jnp.max(a_mat, axis=-1, keepdims=True), m_prev)
        w_intra = jnp.exp(a_mat - r)
        w_inter = jnp.exp(m_prev - r)
        q = q_ref[:, sl]
        k = k_ref[:, sl]
        v = v_ref[:, sl]
        s = _dot_nt(q, k) * w_intra
        c_old = c_ref[h]
        n_old = nm_ref[h:h + 1, :]
        num = _dot(s.astype(BF16), v) + w_inter * _dot_nt(q, c_old.astype(BF16))
        den = (jnp.sum(s, axis=-1, keepdims=True)
               + w_inter * jnp.sum(q.astype(F32) * n_old, axis=-1, keepdims=True))
        m_t = b_col + r
        hid = num / jnp.maximum(jnp.abs(den), jnp.exp(-m_t))
        m_new = m_t[L - 1:L, :]
        b_last = b_col[L - 1:L, :]
        w_state = jnp.exp(a_col + (b_last - m_new))
        decay = jnp.exp(b_last + m_prev - m_new)
        vw = (v.astype(F32) * w_state).astype(BF16)
        c_ref[h] = decay * c_old + _dot_tn(vw, k)
        nm_ref[h:h + 1, :] = decay * n_old + jnp.sum(k.astype(F32) * w_state, axis=0, keepdims=True)
        nm_ref[H_M + h:H_M + h + 1, :] = jnp.broadcast_to(m_new, (1, LANES))
        ym_ref[:, sl] = _rms(o_ref[:, sl] * hid, gout_ref[:, sl]).astype(BF16)
    f_glob = cum + fcarry_ref[...]
    fcol_ref[...] = f_glob
    fcarry_ref[...] = f_glob[L - 1:L, :]


def _mlstm_prompt(mq, mk, mv, mo, gates, g_out_m):
    t = mq.shape[0]
    tok = lambda w: pl.BlockSpec((CHUNK, w), lambda c: (c, 0))
    return pl.pallas_call(
        _mlstm_prompt_kernel,
        grid=(t // CHUNK,),
        in_specs=[tok(W_M), tok(W_M), tok(W_M), tok(W_M), tok(GATE_COLS),
                  pl.BlockSpec((1, W_M), lambda c: (0, 0))],
        out_specs=[
            tok(W_M), tok(GATE_COLS),
            pl.BlockSpec((H_M, D_HM, D_HM), lambda c: (0, 0, 0)),
            pl.BlockSpec((2 * H_M, D_HM), lambda c: (0, 0)),
        ],
        out_shape=(
            jax.ShapeDtypeStruct((t, W_M), BF16), jax.ShapeDtypeStruct((t, GATE_COLS), F32),
            jax.ShapeDtypeStruct((H_M, D_HM, D_HM), F32), jax.ShapeDtypeStruct((2 * H_M, D_HM), F32),
        ),
        scratch_shapes=[pltpu.VMEM((1, GATE_COLS), F32)],
        compiler_params=_params("arbitrary"),
        name="mlstm_prompt",
    )(mq, mk, mv, mo, gates, g_out_m)


def _fox_prep_kernel(q_ref, k_ref, v_ref, fcol_ref, qat_ref, ka_ref, vat_ref):
    tm = q_ref.shape[0]
    lane = lax.broadcasted_iota(jnp.int32, (tm, LANES), 1)
    lo = lane < D_HF
    f_all = fcol_ref[...] * LOG2E
    ones_rows = jnp.where(lax.broadcasted_iota(jnp.int32, (VROWS - D_HF, tm), 0) == 0, 1.0, 0.0).astype(BF16)
    for p in range(H_F // 2):
        sl = slice(p * LANES, (p + 1) * LANES)
        q2 = q_ref[:, sl].astype(F32)
        k_aug = jnp.zeros((tm, LANES), F32)
        for half in range(2):
            h = 2 * p + half
            hi, mid, lw = _split3(f_all[:, FOX_COL + h:FOX_COL + h + 1])
            b = 8 * half
            q_aug = jnp.where(lane == b, hi, jnp.where(lane == b + 1, mid, jnp.where(lane == b + 2, lw, jnp.where(
                (lane >= b + 3) & (lane < b + 6), 1.0, 0.0))))
            k_aug = k_aug + jnp.where((lane >= b) & (lane < b + 3), 1.0, jnp.where(
                lane == b + 3, -hi, jnp.where(lane == b + 4, -mid, jnp.where(lane == b + 5, -lw, 0.0))))
            q_own = jnp.where(lo, q2, 0.0) if half == 0 else jnp.where(lo, 0.0, q2)
            qat_ref[h * AUGW:h * AUGW + LANES, :] = q_own.T.astype(BF16)
            qat_ref[h * AUGW + LANES:(h + 1) * AUGW, :] = q_aug.T.astype(BF16)
        ka_ref[:, p * AUGW:p * AUGW + LANES] = k_ref[:, sl]
        ka_ref[:, p * AUGW + LANES:(p + 1) * AUGW] = k_aug.astype(BF16)
        vt = v_ref[:, sl].astype(F32).T.astype(BF16)
        for half in range(2):
            h = 2 * p + half
            vat_ref[h * VROWS:h * VROWS + D_HF, :] = vt[half * D_HF:(half + 1) * D_HF, :]
            vat_ref[h * VROWS + D_HF:(h + 1) * VROWS, :] = ones_rows


def _fox_prep(fq, fk16, fv16, fcol, tm):
    t = fq.shape[0]
    tok = lambda w: pl.BlockSpec((tm, w), lambda i: (i, 0))
    tr = lambda r: pl.BlockSpec((r, tm), lambda i: (0, i))
    return pl.pallas_call(
        _fox_prep_kernel,
        grid=(t // tm,),
        in_specs=[tok(W_F), tok(W_F), tok(W_F), tok(GATE_COLS)],
        out_specs=[tr(H_F * AUGW), tok(H_F // 2 * AUGW), tr(H_F * VROWS)],
        out_shape=(jax.ShapeDtypeStruct((H_F * AUGW, t), BF16), jax.ShapeDtypeStruct((t, H_F // 2 * AUGW), BF16),
                   jax.ShapeDtypeStruct((H_F * VROWS, t), BF16)),
        compiler_params=_params("arbitrary"),
        name="fox_prep",
    )(fq, fk16, fv16, fcol)


def _fox_prompt_kernel(qtab_ref, ktab_ref, qat_ref, ka_ref, vat_ref, gcol_ref, y_ref, m_scr, acc_scr, *, tq):
    i = pl.program_id(0)
    qb = qtab_ref[i]
    kb = ktab_ref[i]

    @pl.when(kb == 0)
    def _init():
        m_scr[...] = jnp.full_like(m_scr, -jnp.inf)
        acc_scr[...] = jnp.zeros_like(acc_scr)

    def block(diagonal):
        for h in range(H_F):
            p = h // 2
            s = _dot(ka_ref[:, p * AUGW:(p + 1) * AUGW], qat_ref[h * AUGW:(h + 1) * AUGW, :])
            if diagonal:
                keep = (lax.broadcasted_iota(jnp.int32, (tq, tq), 0) <= lax.broadcasted_iota(jnp.int32, (tq, tq), 1))
                s = jnp.where(keep, s, -jnp.inf)
            m_prev = m_scr[h]
            m_new = jnp.maximum(m_prev, jnp.max(s, axis=0, keepdims=True))
            m_scr[h] = m_new
            prob_t = jnp.exp2(s - m_new).astype(BF16)
            acc_scr[h] = jnp.exp2(m_prev - m_new) * acc_scr[h] + _dot(vat_ref[h * VROWS:(h + 1) * VROWS, :], prob_t)
        if diagonal:
            for p in range(H_F // 2):
                outs = []
                for h in (2 * p, 2 * p + 1):
                    acc = acc_scr[h]
                    out = acc[:D_HF, :] / acc[D_HF:D_HF + 1, :]
                    ms = jnp.mean(out * out, axis=0, keepdims=True)
                    outs.append(out * lax.rsqrt(ms + EPS) * gcol_ref[h * D_HF:(h + 1) * D_HF, :])
                y_ref[:, p * LANES:(p + 1) * LANES] = jnp.concatenate(outs, axis=0).T.astype(BF16)

    @pl.when(kb < qb)
    def _off_diagonal():
        block(False)

    @pl.when(kb == qb)
    def _diagonal():
        block(True)


def _fox_prompt(qat, ka, vat, g_col, tq):
    t = ka.shape[0]
    nb = t // tq
    pairs = [(q, k) for q in range(nb) for k in range(q + 1)]
    qtab = jnp.asarray(np.array([p[0] for p in pairs], np.int32))
    ktab = jnp.asarray(np.array([p[1] for p in pairs], np.int32))
    grid_spec = pltpu.PrefetchScalarGridSpec(
        num_scalar_prefetch=2,
        grid=(len(pairs),),
        in_specs=[
            pl.BlockSpec((H_F * AUGW, tq), lambda i, qt, kt: (0, qt[i])),
            pl.BlockSpec((tq, H_F // 2 * AUGW), lambda i, qt, kt: (kt[i], 0)),
            pl.BlockSpec((H_F * VROWS, tq), lambda i, qt, kt: (0, kt[i])),
            pl.BlockSpec((W_F, 1), lambda i, qt, kt: (0, 0)),
        ],
        out_specs=pl.BlockSpec((tq, W_F), lambda i, qt, kt: (qt[i], 0)),
        scratch_shapes=[pltpu.VMEM((H_F, 1, tq), F32), pltpu.VMEM((H_F, VROWS, tq), F32)],
    )
    return pl.pallas_call(
        functools.partial(_fox_prompt_kernel, tq=tq),
        grid_spec=grid_spec,
        out_shape=jax.ShapeDtypeStruct((t, W_F), BF16),
        compiler_params=_params("arbitrary"),
        name="fox_prompt",
    )(qtab, ktab, qat, ka, vat, g_col)


def _mlstm_sample_kernel(q_ref, k_ref, v_ref, o_ref, g_ref, gout_ref, c0_ref, n0_ref, m0_ref,
                         ym_ref, c_ref, n_ref, m_ref):
    R = SAMPLE_ROWS
    gates = g_ref[0]
    rows = [gates[0:1, :]]
    for t in range(1, DEC_SEQ):
        rows.append(rows[-1] + gates[t:t + 1, :])
    rows += [rows[-1]] * (R - DEC_SEQ)
    cum = jnp.concatenate(rows, axis=0)
    ridx = lax.broadcasted_iota(jnp.int32, (R, 1), 0)
    for h in range(H_M):
        sl = slice(h * D_HM, (h + 1) * D_HM)
        b_col = cum[:, H_M + h:H_M + h + 1]
        a_col = gates[:, h:h + 1] - b_col
        m_prev = m0_ref[0, h:h + 1, 0:1]
        cm = jnp.full((R, 1), -jnp.inf, F32)
        for s in range(DEC_SEQ):
            cm = jnp.maximum(cm, jnp.where(ridx >= s, a_col[s:s + 1, :], -jnp.inf))
        r = jnp.maximum(cm, m_prev)
        w_inter = jnp.exp(m_prev - r)
        q = q_ref[0, :, sl]
        k = k_ref[0, :, sl]
        v = v_ref[0, :, sl]
        q32, k32, v32 = q.astype(F32), k.astype(F32), v.astype(F32)
        c_old = c0_ref[0, h]
        n_old = n0_ref[0, h:h + 1, :]
        num = w_inter * _dot_nt(q, c_old.astype(BF16))
        den = w_inter * jnp.sum(q32 * n_old, axis=-1, keepdims=True)
        for s in range(DEC_SEQ):
            w_s = jnp.where(ridx >= s, jnp.exp(a_col[s:s + 1, :] - r), 0.0)
            s_col = jnp.sum(q32 * k32[s:s + 1, :], axis=-1, keepdims=True) * w_s
            num = num + s_col * v32[s:s + 1, :]
            den = den + s_col
        m_t = b_col + r
        hid = num / jnp.maximum(jnp.abs(den), jnp.exp(-m_t))
        m_new = m_t[DEC_SEQ - 1:DEC_SEQ, :]
        b_last = b_col[DEC_SEQ - 1:DEC_SEQ, :]
        w_state = jnp.where(ridx < DEC_SEQ, jnp.exp(a_col + (b_last - m_new)), 0.0)
        decay = jnp.exp(b_last + m_prev - m_new)
        pad = jnp.zeros((D_HM - R, D_HM), BF16)
        vw = jnp.concatenate([(v32 * w_state).astype(BF16), pad], axis=0)
        kp = jnp.concatenate([k, pad], axis=0)
        c_ref[0, h] = decay * c_old + _dot_tn(vw, kp)
        n_ref[0, h:h + 1, :] = decay * n_old + jnp.sum(k32 * w_state, axis=0, keepdims=True)
        m_ref[0, h:h + 1, :] = jnp.broadcast_to(m_new, (1, LANES))
        ym_ref[0, :, sl] = _rms(o_ref[0, :, sl] * hid, gout_ref[:, sl]).astype(BF16)


def _mlstm_sample(mq, mk, mv, mo, gates, g_out_m, c0, n0, m0):
    nb = mq.shape[0]
    seq = lambda w: pl.BlockSpec((1, SAMPLE_ROWS, w), lambda b: (b, 0, 0))
    st_c = pl.BlockSpec((1, H_M, D_HM, D_HM), lambda b: (b, 0, 0, 0))
    st_v = pl.BlockSpec((1, H_M, D_HM), lambda b: (b, 0, 0))
    return pl.pallas_call(
        _mlstm_sample_kernel,
        grid=(nb,),
        in_specs=[seq(W_M), seq(W_M), seq(W_M), seq(W_M), seq(GATE_COLS),
                  pl.BlockSpec((1, W_M), lambda b: (0, 0)), st_c, st_v, st_v],
        out_specs=[seq(W_M), st_c, st_v, st_v],
        out_shape=(
            jax.ShapeDtypeStruct((nb, SAMPLE_ROWS, W_M), BF16),
            jax.ShapeDtypeStruct((nb, H_M, D_HM, D_HM), F32),
            jax.ShapeDtypeStruct((nb, H_M, D_HM), F32), jax.ShapeDtypeStruct((nb, H_M, D_HM), F32),
        ),
        compiler_params=_params("arbitrary"),
        name="mlstm_sample",
    )(mq, mk, mv, mo, gates, g_out_m, c0, n0, m0)


def _fox_sample_kernel(pt_ref, q_ref, knew_ref, vnew_ref, lfnew_ref, gout_ref, *rest, pages, n_steps):
    k_refs = rest[:pages]
    v_refs = rest[pages:2 * pages]
    lf_refs = rest[2 * pages:3 * pages]
    y_ref, m_scr, l_scr, acc_scr, carry_scr = rest[3 * pages:]
    step = pl.program_id(1)
    r = SAMPLE_ROWS

    @pl.when(step == 0)
    def _init():
        m_scr[...] = jnp.full_like(m_scr, -jnp.inf)
        l_scr[...] = jnp.zeros_like(l_scr)
        acc_scr[...] = jnp.zeros_like(acc_scr)
        carry_scr[...] = jnp.zeros_like(carry_scr)

    def absorb(scores, pv):
        s = jnp.concatenate(scores, axis=0)
        m_prev = m_scr[...]
        m_new = jnp.maximum(m_prev, jnp.max(s, axis=-1, keepdims=True))
        alpha = jnp.exp2(m_prev - m_new)
        prob = jnp.exp2(s - m_new)
        l_scr[...] = alpha * l_scr[...] + jnp.sum(prob, axis=-1, keepdims=True)
        pb = prob.astype(BF16)
        acc_scr[...] = alpha * acc_scr[...] + jnp.concatenate(
            [pv(h, pb[h * r:(h + 1) * r, :]) for h in range(H_F)], axis=0)
        m_scr[...] = m_new

    upper = (lax.broadcasted_iota(jnp.int32, (PAGE_SIZE, PAGE_SIZE), 0)
             <= lax.broadcasted_iota(jnp.int32, (PAGE_SIZE, PAGE_SIZE), 1)).astype(F32)
    cum = _dot_hi(jnp.concatenate([lf_refs[j][...] for j in range(pages)], axis=0), upper)
    carry = carry_scr[...]
    f_past = []
    for j in range(pages):
        f_past.append(carry + cum[j * H_F:(j + 1) * H_F, :])
        carry = f_past[-1][:, PAGE_SIZE - 1:PAGE_SIZE]
    carry_scr[...] = carry
    heads = [slice(h * D_HF, (h + 1) * D_HF) for h in range(H_F)]
    scores, v_t = [], []
    for h in range(H_F):
        k_t = jnp.concatenate([k_refs[j][h] for j in range(pages)], axis=1).astype(BF16)
        v_t.append(jnp.concatenate([v_refs[j][h] for j in range(pages)], axis=1).astype(BF16))
        bias = jnp.concatenate([f_past[j][h:h + 1, :] for j in range(pages)], axis=1) * LOG2E
        scores.append(_dot(q_ref[0, :, heads[h]], k_t) - bias)
    absorb(scores, lambda h, prob: _dot_nt(prob, v_t[h]))

    @pl.when(step == n_steps - 1)
    def _finish():
        cn = _dot_hi(lfnew_ref[0], upper)[:, :r]
        causal = lax.broadcasted_iota(jnp.int32, (r, r), 1) <= lax.broadcasted_iota(jnp.int32, (r, r), 0)
        scores = []
        for h in range(H_F):
            bias = (cn[h:h + 1, :] + carry[h:h + 1, :]) * LOG2E
            scores.append(jnp.where(causal, _dot_nt(q_ref[0, :, heads[h]], knew_ref[0, :, heads[h]]) - bias, -jnp.inf))
        absorb(scores, lambda h, prob: _dot(prob, vnew_ref[0, :, heads[h]]))
        out = acc_scr[...] / l_scr[...]
        y_ref[0] = jnp.concatenate(
            [_rms(out[h * r:(h + 1) * r, :], gout_ref[:, heads[h]]) for h in range(H_F)], axis=1).astype(BF16)


def _fox_sample(page_table, fq, fk16, fv16, lf_new_t, g_out_f, cache_k, cache_v, cache_lf_t, pages):
    nb, n_pages = page_table.shape
    n_steps = n_pages // pages
    seq = lambda w: pl.BlockSpec((1, SAMPLE_ROWS, w), lambda b, s, pt: (b, 0, 0))

    def page_of(j):
        return lambda b, s, pt: pt[b * n_pages + s * pages + j]

    def kv_page(j):
        f = page_of(j)
        return pl.BlockSpec((None, H_F, D_HF, PAGE_SIZE), lambda b, s, pt: (f(b, s, pt), 0, 0, 0))

    def lf_page(j):
        f = page_of(j)
        return pl.BlockSpec((None, H_F, PAGE_SIZE), lambda b, s, pt: (f(b, s, pt), 0, 0))

    grid_spec = pltpu.PrefetchScalarGridSpec(
        num_scalar_prefetch=1,
        grid=(nb, n_steps),
        in_specs=([seq(W_F), seq(W_F), seq(W_F),
                   pl.BlockSpec((1, H_F, PAGE_SIZE), lambda b, s, pt: (b, 0, 0)),
                   pl.BlockSpec((1, W_F), lambda b, s, pt: (0, 0))]
                  + [kv_page(j) for j in range(pages)] + [kv_page(j) for j in range(pages)]
                  + [lf_page(j) for j in range(pages)]),
        out_specs=seq(W_F),
        scratch_shapes=[pltpu.VMEM((H_F * SAMPLE_ROWS, 1), F32), pltpu.VMEM((H_F * SAMPLE_ROWS, 1), F32),
                        pltpu.VMEM((H_F * SAMPLE_ROWS, D_HF), F32), pltpu.VMEM((H_F, 1), F32)],
    )
    return pl.pallas_call(
        functools.partial(_fox_sample_kernel, pages=pages, n_steps=n_steps),
        grid_spec=grid_spec,
        out_shape=jax.ShapeDtypeStruct((nb, SAMPLE_ROWS, W_F), BF16),
        compiler_params=_params("arbitrary", "arbitrary"),
        name="fox_sample",
    )(page_table.reshape(-1), fq, fk16, fv16, lf_new_t, g_out_f,
      *([cache_k] * pages), *([cache_v] * pages), *([cache_lf_t] * pages))


def _post_kernel(x_ref, ym_ref, yf_ref, gt1_ref, sh2_ref, sc2_ref, gt2_ref, g2_ref, gfin_ref,
                 wom_ref, wof_ref, wr_ref, br_ref, wg_ref, wu_ref, wd_ref,
                 o_ref, xp_scr, hb_scr, comb_scr, acc_scr):
    e = pl.program_id(1)

    @pl.when(e == 0)
    def _prepare():
        mix = _dot(ym_ref[...], wom_ref[...]) + _dot(yf_ref[...], wof_ref[...])
        xp = x_ref[...] + gt1_ref[...] * mix
        xp_scr[...] = xp
        h = _rms(xp, g2_ref[...]) * (1.0 + sc2_ref[...]) + sh2_ref[...]
        hb, h_lo = _split_bf16(h)
        hb_scr[...] = hb
        rr = _dot(hb, wr_ref[...])
        logits = rr[:, :LANES] + rr[:, LANES:] + _dot(h_lo, wr_ref[:, :LANES]) + br_ref[...]
        lane = lax.broadcasted_iota(jnp.int32, logits.shape, 1)
        big = jnp.int32(4 * LANES)
        gl = jnp.where(lane < N_GROUPS, logits, -jnp.inf)
        gmax = jnp.max(gl, axis=-1, keepdims=True)
        gidx = jnp.min(jnp.where(gl == gmax, lane, big), axis=-1, keepdims=True)
        g_w = 1.0 / jnp.sum(jnp.exp(gl - gmax), axis=-1, keepdims=True)
        first = N_GROUPS + EXP_PER_GROUP * gidx
        el = jnp.where((lane >= first) & (lane < first + EXP_PER_GROUP), logits, -jnp.inf)
        emax = jnp.max(el, axis=-1, keepdims=True)
        pe = jnp.exp(el - emax)
        prob = pe / jnp.sum(pe, axis=-1, keepdims=True)
        v1 = jnp.max(prob, axis=-1, keepdims=True)
        i1 = jnp.min(jnp.where(prob == v1, lane, big), axis=-1, keepdims=True)
        rest = jnp.where((lane == i1) | (el == -jnp.inf), -1.0, prob)
        v2 = jnp.max(rest, axis=-1, keepdims=True)
        i2 = jnp.min(jnp.where(rest == v2, lane, big), axis=-1, keepdims=True)
        tot = v1 + v2
        comb_scr[...] = jnp.where(lane == i1, v1 / tot, jnp.where(lane == i2, v2 / tot, 0.0)) * g_w
        acc_scr[...] = jnp.zeros_like(acc_scr)

    hb = hb_scr[...]
    a = _dot(hb, wg_ref[0])
    u = _dot(hb, wu_ref[0])
    lane = lax.broadcasted_iota(jnp.int32, comb_scr.shape, 1)
    ce = jnp.sum(jnp.where(lane == e + N_GROUPS, comb_scr[...], 0.0), axis=-1, keepdims=True)
    act = (a * jax.nn.sigmoid(a) * u * ce).astype(BF16)
    acc_scr[...] += _dot(act, wd_ref[0])

    @pl.when(e == N_EXPERTS - 1)
    def _finish():
        o_ref[...] = _rms(xp_scr[...] + gt2_ref[...] * acc_scr[...], gfin_ref[...])


def _post(x, ym, yf, gt1, sh2, sc2, gt2, g_norm2, g_final, wo_m, wo_f, w_route2, b_route, wg, wu, wd, tm):
    t = x.shape[0]
    mod_rows = gt1.shape[0]
    mod_map = (lambda i, e: (0, 0)) if mod_rows == 1 else (lambda i, e: (i, 0))
    mod = pl.BlockSpec((1, D_MODEL) if mod_rows == 1 else (tm, D_MODEL), mod_map)
    tok = lambda w: pl.BlockSpec((tm, w), lambda i, e: (i, 0))
    const = lambda shape: pl.BlockSpec(shape, lambda i, e: (0,) * len(shape))
    return pl.pallas_call(
        _post_kernel,
        grid=(t // tm, N_EXPERTS),
        in_specs=[
            tok(D_MODEL), tok(W_M), tok(W_F), mod, mod, mod, mod,
            const((1, D_MODEL)), const((1, D_MODEL)),
            const(wo_m.shape), const(wo_f.shape), const(w_route2.shape), const((1, LANES)),
            pl.BlockSpec((1, D_MODEL, D_FF_E), lambda i, e: (e, 0, 0)),
            pl.BlockSpec((1, D_MODEL, D_FF_E), lambda i, e: (e, 0, 0)),
            pl.BlockSpec((1, D_FF_E, D_MODEL), lambda i, e: (e, 0, 0)),
        ],
        out_specs=tok(D_MODEL),
        out_shape=jax.ShapeDtypeStruct((t, D_MODEL), F32),
        scratch_shapes=[pltpu.VMEM((tm, D_MODEL), F32), pltpu.VMEM((tm, D_MODEL), BF16),
                        pltpu.VMEM((tm, LANES), F32), pltpu.VMEM((tm, D_MODEL), F32)],
        compiler_params=_params("arbitrary", "arbitrary"),
        name="post",
    )(x, ym, yf, gt1, sh2, sc2, gt2, g_norm2, g_final, wo_m, wo_f, w_route2, b_route, wg, wu, wd)


def _pad_cols(w, n):
    return jnp.pad(w, ((0, 0), (0, n - w.shape[1])))


def _hi_lo_cols(w):
    hi, lo = _split_bf16(w)
    return jnp.concatenate([hi, lo], axis=1)


def _pad_seq(a):
    nb = a.shape[0] // DEC_SEQ
    return jnp.pad(a.reshape(nb, DEC_SEQ, a.shape[1]), ((0, 0), (0, SAMPLE_ROWS - DEC_SEQ), (0, 0)))


def _unpad_seq(a):
    return a[:, :DEC_SEQ].reshape(a.shape[0] * DEC_SEQ, a.shape[2])


def kernel(x_prompt, x_sample, c_prompt, c_sample, cache_k, cache_v, cache_logf, state_C, state_n, state_m,
           page_table, w_ada, b_ada, g_norm1, g_norm2, w_in, b_ig, b_fg, b_ff, g_out_m, g_out_f, w_out,
           w_group, b_group, w_router, b_router, w_gate, w_up, w_down, g_final):
    depth = w_ada.shape[0]
    assert depth == 1 and x_prompt.shape[0] == 1
    t = x_prompt.shape[1]
    nb, ds = x_sample.shape[:2]
    assert ds == DEC_SEQ
    ts = nb * ds
    xp = x_prompt.reshape(t, D_MODEL)
    xs = x_sample.reshape(ts, D_MODEL)
    l = 0

    w = w_in[l]
    o = np.cumsum([0, W_M, W_M, W_M, W_M, H_M, H_M, W_F, W_F, W_F, H_F])
    w_main = jnp.concatenate([w[:, o[0]:o[4]], w[:, o[6]:o[9]]], axis=1).astype(BF16)
    w_gates = _pad_cols(jnp.concatenate([w[:, o[4]:o[6]], w[:, o[9]:o[10]]], axis=1), GATE_COLS)
    w_gate2 = _hi_lo_cols(w_gates)
    gate_bias = _pad_cols(jnp.concatenate([b_ig[l], b_fg[l], b_ff[l]])[None, :], GATE_COLS)
    gm = g_out_m[l].reshape(1, W_M)
    gf = g_out_f[l].reshape(1, W_F)
    wo_m = w_out[l][:W_M].astype(BF16)
    wo_f = w_out[l][W_M:].astype(BF16)
    w_route = jnp.concatenate(
        [w_group[l], jnp.transpose(w_router[l], (1, 0, 2)).reshape(D_MODEL, N_EXPERTS)], axis=1)
    w_route2 = _hi_lo_cols(_pad_cols(w_route, LANES))
    b_route = _pad_cols(jnp.concatenate([b_group[l], b_router[l].reshape(-1)])[None, :], LANES)
    wg = w_gate[l].reshape(N_EXPERTS, D_MODEL, D_FF_E).astype(BF16)
    wu = w_up[l].reshape(N_EXPERTS, D_MODEL, D_FF_E).astype(BF16)
    wd = w_down[l].reshape(N_EXPERTS, D_FF_E, D_MODEL).astype(BF16)

    rows = -(-(1 + nb) // 8) * 8
    c_all = jnp.pad(jnp.concatenate([c_prompt, c_sample], axis=0), ((0, rows - 1 - nb), (0, 0)))
    mod = _adaln(c_all, w_ada[l], b_ada[l])
    p_mod = [mod[0:1, i * D_MODEL:(i + 1) * D_MODEL] for i in range(6)]
    s_mod = [jnp.repeat(mod[1:1 + nb, i * D_MODEL:(i + 1) * D_MODEL], ds, axis=0) for i in range(6)]
    g1 = g_norm1[l][None, :]
    g2 = g_norm2[l][None, :]
    gfin = g_final[None, :]

    tm = min(512, t)
    tq = min(1024, t)
    mq, mk, mv, mo, fq, fk, fv, fk16, fv16, gates = _inproj(xp, p_mod[0], p_mod[1], g1, w_main, w_gate2, gate_bias, tm)
    ym, fcol, c_p, nm_p = _mlstm_prompt(mq, mk, mv, mo, gates, gm)
    qat, ka, vat = _fox_prep(fq, fk16, fv16, fcol, tm)
    yf = _fox_prompt(qat, ka, vat, gf.reshape(W_F, 1), tq)
    y_prompt = _post(xp, ym, yf, p_mod[2], p_mod[3], p_mod[4], p_mod[5], g2, gfin,
                     wo_m, wo_f, w_route2, b_route, wg, wu, wd, tm)

    smq, smk, smv, smo, sfq, sfk, sfv, sfk16, sfv16, sgates = _inproj(
        xs, s_mod[0], s_mod[1], g1, w_main, w_gate2, gate_bias, ts)
    m0 = jnp.broadcast_to(state_m[l][:, :, None], (nb, H_M, LANES))
    sym, c_s, n_s, m_s = _mlstm_sample(_pad_seq(smq), _pad_seq(smk), _pad_seq(smv), _pad_seq(smo), _pad_seq(sgates),
                                       gm, state_C[l], state_n[l], m0)
    lf_new = sgates[:, FOX_COL:FOX_COL + H_F]
    lf_new_t = jnp.pad(jnp.transpose(lf_new.reshape(nb, ds, H_F), (0, 2, 1)), ((0, 0), (0, 0), (0, PAGE_SIZE - ds)))
    syf = _fox_sample(page_table, _pad_seq(sfq), _pad_seq(sfk16), _pad_seq(sfv16), lf_new_t, gf,
                      jnp.transpose(cache_k[l], (0, 2, 3, 1)), jnp.transpose(cache_v[l], (0, 2, 3, 1)),
                      jnp.transpose(cache_logf[l], (0, 2, 1)), pages=FOX_SAMPLE_PAGES)
    y_sample = _post(xs, _unpad_seq(sym), _unpad_seq(syf), s_mod[2], s_mod[3], s_mod[4], s_mod[5], g2, gfin,
                     wo_m, wo_f, w_route2, b_route, wg, wu, wd, ts)

    return (
        y_prompt.reshape(1, t, D_MODEL), y_sample.reshape(nb, ds, D_MODEL),
        fk.reshape(1, 1, t, H_F, D_HF), fv.reshape(1, 1, t, H_F, D_HF),
        gates[:, FOX_COL:FOX_COL + H_F].reshape(1, 1, t, H_F),
        c_p[None, None], nm_p[None, None, :H_M], nm_p[None, None, H_M:, 0],
        sfk.reshape(1, nb, ds, H_F, D_HF), sfv.reshape(1, nb, ds, H_F, D_HF), lf_new.reshape(1, nb, ds, H_F),
        c_s[None], n_s[None], m_s[None, :, :, 0],
    )
```

```python
import functools
import math

import jax
import jax.numpy as jnp
import numpy as np
from jax import lax
from jax.experimental import pallas as pl
from jax.experimental.pallas import tpu as pltpu

F32 = jnp.float32
BF16 = jnp.bfloat16
HIGHEST = lax.Precision.HIGHEST

D_MODEL = 1024
H_M = 4
D_HM = 128
W_M = H_M * D_HM
H_F = 8
D_HF = 64
W_F = H_F * D_HF
PAGE_SIZE = 128
CHUNK = 128
DEC_SEQ = 4
N_GROUPS = 4
EXP_PER_GROUP = 4
N_EXPERTS = N_GROUPS * EXP_PER_GROUP
D_FF_E = D_MODEL // 4
FOX_SCALE = D_HF ** -0.5
LOG2E = math.log2(math.e)
EPS = 1e-6

LANES = 128
BF16_SUBLANES = 16
MXU_DIM = 256
GATE_COLS = LANES
FOX_COL = 2 * H_M
SAMPLE_ROWS = BF16_SUBLANES
AUGW = MXU_DIM
VROWS = D_HF + BF16_SUBLANES
TOKEN_TILE = 512
POST_TILE = 1024
FOX_BLOCK = 1024
STATS_ROWS = 16
SKIP_BELOW = 160.0
FOX_SAMPLE_PAGES = 16
VMEM_LIMIT = 56 * 1024 * 1024


def _params(*sem):
    return pltpu.CompilerParams(dimension_semantics=sem, vmem_limit_bytes=VMEM_LIMIT)


def _dot(a, b):
    return jnp.dot(a, b, preferred_element_type=F32)


def _dot_nt(a, b):
    return lax.dot_general(a, b, (((1,), (1,)), ((), ())), preferred_element_type=F32)


def _dot_tn(a, b):
    return lax.dot_general(a, b, (((0,), (0,)), ((), ())), preferred_element_type=F32)


def _dot_hi(a, b):
    return jnp.dot(a, b, preferred_element_type=F32, precision=HIGHEST)


def _rms(x, g):
    return x * lax.rsqrt(jnp.mean(x * x, axis=-1, keepdims=True) + EPS) * g


def _log_sigmoid(x):
    return jnp.minimum(x, 0.0) - jnp.log1p(jnp.exp(-jnp.abs(x)))


def _split_bf16(x):
    hi = x.astype(BF16)
    lo = (x - hi.astype(F32)).astype(BF16)
    return hi, lo


def _split3(x):
    hi = x.astype(BF16).astype(F32)
    mid = (x - hi).astype(BF16).astype(F32)
    lo = (x - hi - mid).astype(BF16).astype(F32)
    return hi, mid, lo


def _adaln_kernel(c_ref, w_ref, b_ref, o_ref):
    c = c_ref[...]
    o_ref[...] = _dot_hi(c * jax.nn.sigmoid(c), w_ref[...]) + b_ref[...]


def _adaln(c_all, w_ada, b_ada):
    rows = c_all.shape[0]
    n = w_ada.shape[1]
    tn = 1536
    return pl.pallas_call(
        _adaln_kernel,
        grid=(n // tn,),
        in_specs=[
            pl.BlockSpec((rows, D_MODEL), lambda j: (0, 0)),
            pl.BlockSpec((D_MODEL, tn), lambda j: (0, j)),
            pl.BlockSpec((1, tn), lambda j: (0, j)),
        ],
        out_specs=pl.BlockSpec((rows, tn), lambda j: (0, j)),
        out_shape=jax.ShapeDtypeStruct((rows, n), F32),
        compiler_params=_params("arbitrary"),
        name="adaln",
    )(c_all, w_ada, b_ada.reshape(1, n))


def _inproj_kernel(x_ref, shift_ref, scale_ref, g_ref, w_ref, wg_ref, gb_ref,
                   mq_ref, mk_ref, mv_ref, mo_ref, fq_ref, fk_ref, fv_ref, fk16_ref, fv16_ref, gate_ref):
    h = _rms(x_ref[...], g_ref[...]) * (1.0 + scale_ref[...]) + shift_ref[...]
    hb, h_lo = _split_bf16(h)

    def proj(j):
        return _dot(hb, w_ref[:, j * W_M:(j + 1) * W_M])

    mq_ref[...] = proj(0).astype(BF16)
    mk_ref[...] = (proj(1) * (D_HM ** -0.5)).astype(BF16)
    mv_ref[...] = proj(2).astype(BF16)
    mo_ref[...] = jax.nn.sigmoid(proj(3))
    fq_ref[...] = (proj(4) * (FOX_SCALE * LOG2E)).astype(BF16)
    fk = proj(5)
    fk_ref[...] = fk
    fk16_ref[...] = fk.astype(BF16)
    fv = proj(6)
    fv_ref[...] = fv
    fv16_ref[...] = fv.astype(BF16)
    gg = _dot(hb, wg_ref[...])
    gpre = gg[:, :GATE_COLS] + gg[:, GATE_COLS:] + _dot(h_lo, wg_ref[:, :GATE_COLS]) + gb_ref[...]
    lane = lax.broadcasted_iota(jnp.int32, gpre.shape, 1)
    gate_ref[...] = jnp.where(lane < H_M, gpre, jnp.where(lane < FOX_COL + H_F, _log_sigmoid(gpre), 0.0))


def _inproj(x, shift, scale, g_norm, w_main, w_gate2, gate_bias, tm):
    t = x.shape[0]
    mod_rows = shift.shape[0]
    mod_map = (lambda i: (0, 0)) if mod_rows == 1 else (lambda i: (i, 0))
    mod_block = (1, D_MODEL) if mod_rows == 1 else (tm, D_MODEL)
    tok = lambda w: pl.BlockSpec((tm, w), lambda i: (i, 0))
    const = lambda shape: pl.BlockSpec(shape, lambda i: (0, 0))
    out_shapes = (
        jax.ShapeDtypeStruct((t, W_M), BF16), jax.ShapeDtypeStruct((t, W_M), BF16),
        jax.ShapeDtypeStruct((t, W_M), BF16), jax.ShapeDtypeStruct((t, W_M), F32),
        jax.ShapeDtypeStruct((t, W_F), BF16), jax.ShapeDtypeStruct((t, W_F), F32),
        jax.ShapeDtypeStruct((t, W_F), F32), jax.ShapeDtypeStruct((t, W_F), BF16),
        jax.ShapeDtypeStruct((t, W_F), BF16), jax.ShapeDtypeStruct((t, GATE_COLS), F32),
    )
    return pl.pallas_call(
        _inproj_kernel,
        grid=(t // tm,),
        in_specs=[
            tok(D_MODEL),
            pl.BlockSpec(mod_block, mod_map), pl.BlockSpec(mod_block, mod_map),
            const((1, D_MODEL)), const(w_main.shape), const(w_gate2.shape), const((1, GATE_COLS)),
        ],
        out_specs=[tok(W_M)] * 4 + [tok(W_F)] * 5 + [tok(GATE_COLS)],
        out_shape=out_shapes,
        compiler_params=_params("arbitrary"),
        name="inproj",
    )(x, shift, scale, g_norm, w_main, w_gate2, gate_bias)


def _mlstm_prompt_kernel(q_ref, k_ref, v_ref, o_ref, g_ref, gout_ref,
                         ym_ref, fcol_ref, c_ref, nm_ref, fcarry_ref):
    L = CHUNK

    @pl.when(pl.program_id(0) == 0)
    def _init():
        c_ref[...] = jnp.zeros_like(c_ref)
        nm_ref[...] = jnp.zeros_like(nm_ref)
        fcarry_ref[...] = jnp.zeros_like(fcarry_ref)

    row = lax.broadcasted_iota(jnp.int32, (L, L), 0)
    col = lax.broadcasted_iota(jnp.int32, (L, L), 1)
    causal = row >= col
    gates = g_ref[...]
    cum = _dot_hi(causal.astype(F32), gates)
    gates_t = gates.T
    cum_t = cum.T
    for h in range(H_M):
        sl = slice(h * D_HM, (h + 1) * D_HM)
        a_row = gates_t[h:h + 1, :] - cum_t[H_M + h:H_M + h + 1, :]
        b_col = cum[:, H_M + h:H_M + h + 1]
        a_col = gates[:, h:h + 1] - b_col
        m_prev = nm_ref[H_M + h:H_M + h + 1, 0:1]
        a_mat = jnp.where(causal, a_row, -jnp.inf)
        r = jnp.maximum(jnp.max(a_mat, axis=-1, keepdims=True), m_prev)
        w_intra = jnp.exp(a_mat - r)
        w_inter = jnp.exp(m_prev - r)
        q = q_ref[:, sl]
        k = k_ref[:, sl]
        v = v_ref[:, sl]
        s = _dot_nt(q, k) * w_intra
        c_old = c_ref[h]
        n_old = nm_ref[h:h + 1, :]
        num = _dot(s.astype(BF16), v) + w_inter * _dot_nt(q, c_old.astype(BF16))
        den = (jnp.sum(s, axis=-1, keepdims=True)
               + w_inter * jnp.sum(q.astype(F32) * n_old, axis=-1, keepdims=True))
        m_t = b_col + r
        hid = num / jnp.maximum(jnp.abs(den), jnp.exp(-m_t))
        m_new = m_t[L - 1:L, :]
        b_last = b_col[L - 1:L, :]
        w_state = jnp.exp(a_col + (b_last - m_new))
        decay = jnp.exp(b_last + m_prev - m_new)
        vw = (v.astype(F32) * w_state).astype(BF16)
        c_ref[h] = decay * c_old + _dot_tn(vw, k)
        nm_ref[h:h + 1, :] = decay * n_old + jnp.sum(k.astype(F32) * w_state, axis=0, keepdims=True)
        nm_ref[H_M + h:H_M + h + 1, :] = jnp.broadcast_to(m_new, (1, LANES))
        ym_ref[:, sl] = _rms(o_ref[:, sl] * hid, gout_ref[:, sl]).astype(BF16)
    f_glob = cum + fcarry_ref[...]
    fcol_ref[...] = f_glob
    fcarry_ref[...] = f_glob[L - 1:L, :]


def _mlstm_prompt(mq, mk, mv, mo, gates, g_out_m):
    t = mq.shape[0]
    tok = lambda w: pl.BlockSpec((CHUNK, w), lambda c: (c, 0))
    return pl.pallas_call(
        _mlstm_prompt_kernel,
        grid=(t // CHUNK,),
        in_specs=[tok(W_M), tok(W_M), tok(W_M), tok(W_M), tok(GATE_COLS),
                  pl.BlockSpec((1, W_M), lambda c: (0, 0))],
        out_specs=[
            tok(W_M), tok(GATE_COLS),
            pl.BlockSpec((H_M, D_HM, D_HM), lambda c: (0, 0, 0)),
            pl.BlockSpec((2 * H_M, D_HM), lambda c: (0, 0)),
        ],
        out_shape=(
            jax.ShapeDtypeStruct((t, W_M), BF16), jax.ShapeDtypeStruct((t, GATE_COLS), F32),
            jax.ShapeDtypeStruct((H_M, D_HM, D_HM), F32), jax.ShapeDtypeStruct((2 * H_M, D_HM), F32),
        ),
        scratch_shapes=[pltpu.VMEM((1, GATE_COLS), F32)],
        compiler_params=_params("arbitrary"),
        name="mlstm_prompt",
    )(mq, mk, mv, mo, gates, g_out_m)


def _fox_prep_kernel(q_ref, k_ref, v_ref, fcol_ref, qat_ref, ka_ref, vat_ref, stats_ref):
    tm = q_ref.shape[0]
    lane = lax.broadcasted_iota(jnp.int32, (tm, LANES), 1)
    lo = lane < D_HF
    f_all = fcol_ref[...] * LOG2E
    ones_rows = jnp.where(lax.broadcasted_iota(jnp.int32, (VROWS - D_HF, tm), 0) == 0, 1.0, 0.0).astype(BF16)
    half_of_lane = lax.broadcasted_iota(jnp.int32, (LANES, LANES), 0) // D_HF
    half_sum = (half_of_lane == lax.broadcasted_iota(jnp.int32, (LANES, LANES), 1)).astype(BF16)
    q_norms, k_norms = [], []
    for p in range(H_F // 2):
        sl = slice(p * LANES, (p + 1) * LANES)
        q2 = q_ref[:, sl].astype(F32)
        k2 = k_ref[:, sl].astype(F32)
        q_norms.append(jnp.max(_dot((q2 * q2).astype(BF16), half_sum), axis=0, keepdims=True))
        k_norms.append(jnp.max(_dot((k2 * k2).astype(BF16), half_sum), axis=0, keepdims=True))
        k_aug = jnp.zeros((tm, LANES), F32)
        for half in range(2):
            h = 2 * p + half
            hi, mid, lw = _split3(f_all[:, FOX_COL + h:FOX_COL + h + 1])
            b = 8 * half
            q_aug = jnp.where(lane == b, hi, jnp.where(lane == b + 1, mid, jnp.where(lane == b + 2, lw, jnp.where(
                (lane >= b + 3) & (lane < b + 6), 1.0, 0.0))))
            k_aug = k_aug + jnp.where((lane >= b) & (lane < b + 3), 1.0, jnp.where(
                lane == b + 3, -hi, jnp.where(lane == b + 4, -mid, jnp.where(lane == b + 5, -lw, 0.0))))
            q_own = jnp.where(lo, q2, 0.0) if half == 0 else jnp.where(lo, 0.0, q2)
            qat_ref[h * AUGW:h * AUGW + LANES, :] = q_own.T.astype(BF16)
            qat_ref[h * AUGW + LANES:(h + 1) * AUGW, :] = q_aug.T.astype(BF16)
        ka_ref[:, p * AUGW:p * AUGW + LANES] = k_ref[:, sl]
        ka_ref[:, p * AUGW + LANES:(p + 1) * AUGW] = k_aug.astype(BF16)
        vt = v_ref[:, sl].astype(F32).T.astype(BF16)
        for half in range(2):
            h = 2 * p + half
            vat_ref[h * VROWS:h * VROWS + D_HF, :] = vt[half * D_HF:(half + 1) * D_HF, :]
            vat_ref[h * VROWS + D_HF:(h + 1) * VROWS, :] = ones_rows
    pad = jnp.zeros((STATS_ROWS - H_F - 2, LANES), F32)
    stats_ref[0] = jnp.concatenate(q_norms + k_norms + [f_all[0:1, :], f_all[tm - 1:tm, :], pad], axis=0)


def _fox_prep(fq, fk16, fv16, fcol, tm):
    t = fq.shape[0]
    tok = lambda w: pl.BlockSpec((tm, w), lambda i: (i, 0))
    tr = lambda r: pl.BlockSpec((r, tm), lambda i: (0, i))
    return pl.pallas_call(
        _fox_prep_kernel,
        grid=(t // tm,),
        in_specs=[tok(W_F), tok(W_F), tok(W_F), tok(GATE_COLS)],
        out_specs=[tr(H_F * AUGW), tok(H_F // 2 * AUGW), tr(H_F * VROWS),
                   pl.BlockSpec((1, STATS_ROWS, LANES), lambda i: (i, 0, 0))],
        out_shape=(jax.ShapeDtypeStruct((H_F * AUGW, t), BF16), jax.ShapeDtypeStruct((t, H_F // 2 * AUGW), BF16),
                   jax.ShapeDtypeStruct((H_F * VROWS, t), BF16), jax.ShapeDtypeStruct((t // tm, STATS_ROWS, LANES), F32)),
        compiler_params=_params("arbitrary"),
        name="fox_prep",
    )(fq, fk16, fv16, fcol)


def _fox_prompt_kernel(qtab_ref, ktab_ref, live_ref, kfetch_ref, qat_ref, ka_ref, vat_ref, gcol_ref,
                       y_ref, m_scr, acc_scr, *, tq):
    i = pl.program_id(0)
    qb = qtab_ref[i]
    kb = ktab_ref[i]

    @pl.when(kb == 0)
    def _init():
        m_scr[...] = jnp.full_like(m_scr, -jnp.inf)
        acc_scr[...] = jnp.zeros_like(acc_scr)

    def head_update(h, diagonal):
        p = h // 2
        s = _dot(ka_ref[:, p * AUGW:(p + 1) * AUGW], qat_ref[h * AUGW:(h + 1) * AUGW, :])
        if diagonal:
            keep = (lax.broadcasted_iota(jnp.int32, (tq, tq), 0) <= lax.broadcasted_iota(jnp.int32, (tq, tq), 1))
            s = jnp.where(keep, s, -jnp.inf)
        m_prev = m_scr[h]
        m_new = jnp.maximum(m_prev, jnp.max(s, axis=0, keepdims=True))
        m_scr[h] = m_new
        prob_t = jnp.exp2(s - m_new).astype(BF16)
        acc_scr[h] = jnp.exp2(m_prev - m_new) * acc_scr[h] + _dot(vat_ref[h * VROWS:(h + 1) * VROWS, :], prob_t)

    @pl.when(kb < qb)
    def _off_diagonal():
        for h in range(H_F):
            @pl.when(live_ref[i * H_F + h] != 0)
            def _live_head():
                head_update(h, False)

    @pl.when(kb == qb)
    def _diagonal():
        for h in range(H_F):
            head_update(h, True)
        for p in range(H_F // 2):
            outs = []
            for h in (2 * p, 2 * p + 1):
                acc = acc_scr[h]
                out = acc[:D_HF, :] / acc[D_HF:D_HF + 1, :]
                ms = jnp.mean(out * out, axis=0, keepdims=True)
                outs.append(out * lax.rsqrt(ms + EPS) * gcol_ref[h * D_HF:(h + 1) * D_HF, :])
            y_ref[:, p * LANES:(p + 1) * LANES] = jnp.concatenate(outs, axis=0).T.astype(BF16)


def _fox_live(stats, qtab, ktab):
    slack = 1.0 + 2.0 ** -6
    qn = jnp.sqrt(stats[:, 0:H_F // 2, 0:2].reshape(-1, H_F) * slack)
    kn = jnp.sqrt(stats[:, H_F // 2:H_F, 0:2].reshape(-1, H_F) * slack)
    f_first = stats[:, H_F, FOX_COL:FOX_COL + H_F]
    f_last = stats[:, H_F + 1, FOX_COL:FOX_COL + H_F]
    bound = qn[qtab] * (kn[ktab] + kn[qtab]) + f_first[qtab] - f_last[ktab]
    dead = (bound < -SKIP_BELOW) & (ktab < qtab - 1)[:, None]
    return jnp.logical_not(dead).astype(jnp.int32).reshape(-1)


def _fox_prompt(qat, ka, vat, stats, g_col, tq):
    t = ka.shape[0]
    nb = t // tq
    pairs = [(q, k) for q in range(nb) for k in range(q + 1)]
    qtab = np.array([p[0] for p in pairs], np.int32)
    ktab = np.array([p[1] for p in pairs], np.int32)
    live = _fox_live(stats, qtab, ktab)
    fetch = jnp.any(live.reshape(len(pairs), H_F) != 0, axis=1) | (ktab == qtab)
    kfetch = jnp.asarray(ktab)[lax.cummax(jnp.where(fetch, jnp.arange(len(pairs)), 0))]
    grid_spec = pltpu.PrefetchScalarGridSpec(
        num_scalar_prefetch=4,
        grid=(len(pairs),),
        in_specs=[
            pl.BlockSpec((H_F * AUGW, tq), lambda i, qt, kt, lv, kf: (0, qt[i])),
            pl.BlockSpec((tq, H_F // 2 * AUGW), lambda i, qt, kt, lv, kf: (kf[i], 0)),
            pl.BlockSpec((H_F * VROWS, tq), lambda i, qt, kt, lv, kf: (0, kf[i])),
            pl.BlockSpec((W_F, 1), lambda i, qt, kt, lv, kf: (0, 0)),
        ],
        out_specs=pl.BlockSpec((tq, W_F), lambda i, qt, kt, lv, kf: (qt[i], 0)),
        scratch_shapes=[pltpu.VMEM((H_F, 1, tq), F32), pltpu.VMEM((H_F, VROWS, tq), F32)],
    )
    return pl.pallas_call(
        functools.partial(_fox_prompt_kernel, tq=tq),
        grid_spec=grid_spec,
        out_shape=jax.ShapeDtypeStruct((t, W_F), BF16),
        compiler_params=_params("arbitrary"),
        name="fox_prompt",
    )(jnp.asarray(qtab), jnp.asarray(ktab), live, kfetch, qat, ka, vat, g_col)


def _mlstm_sample_kernel(q_ref, k_ref, v_ref, o_ref, g_ref, gout_ref, c0_ref, n0_ref, m0_ref,
                         ym_ref, c_ref, n_ref, m_ref):
    R = SAMPLE_ROWS
    gates = g_ref[0]
    rows = [gates[0:1, :]]
    for t in range(1, DEC_SEQ):
        rows.append(rows[-1] + gates[t:t + 1, :])
    rows += [rows[-1]] * (R - DEC_SEQ)
    cum = jnp.concatenate(rows, axis=0)
    ridx = lax.broadcasted_iota(jnp.int32, (R, 1), 0)
    for h in range(H_M):
        sl = slice(h * D_HM, (h + 1) * D_HM)
        b_col = cum[:, H_M + h:H_M + h + 1]
        a_col = gates[:, h:h + 1] - b_col
        m_prev = m0_ref[0, h:h + 1, 0:1]
        cm = jnp.full((R, 1), -jnp.inf, F32)
        for s in range(DEC_SEQ):
            cm = jnp.maximum(cm, jnp.where(ridx >= s, a_col[s:s + 1, :], -jnp.inf))
        r = jnp.maximum(cm, m_prev)
        w_inter = jnp.exp(m_prev - r)
        q = q_ref[0, :, sl]
        k = k_ref[0, :, sl]
        v = v_ref[0, :, sl]
        q32, k32, v32 = q.astype(F32), k.astype(F32), v.astype(F32)
        c_old = c0_ref[0, h]
        n_old = n0_ref[0, h:h + 1, :]
        num = w_inter * _dot_nt(q, c_old.astype(BF16))
        den = w_inter * jnp.sum(q32 * n_old, axis=-1, keepdims=True)
        for s in range(DEC_SEQ):
            w_s = jnp.where(ridx >= s, jnp.exp(a_col[s:s + 1, :] - r), 0.0)
            s_col = jnp.sum(q32 * k32[s:s + 1, :], axis=-1, keepdims=True) * w_s
            num = num + s_col * v32[s:s + 1, :]
            den = den + s_col
        m_t = b_col + r
        hid = num / jnp.maximum(jnp.abs(den), jnp.exp(-m_t))
        m_new = m_t[DEC_SEQ - 1:DEC_SEQ, :]
        b_last = b_col[DEC_SEQ - 1:DEC_SEQ, :]
        w_state = jnp.where(ridx < DEC_SEQ, jnp.exp(a_col + (b_last - m_new)), 0.0)
        decay = jnp.exp(b_last + m_prev - m_new)
        pad = jnp.zeros((D_HM - R, D_HM), BF16)
        vw = jnp.concatenate([(v32 * w_state).astype(BF16), pad], axis=0)
        kp = jnp.concatenate([k, pad], axis=0)
        c_ref[0, h] = decay * c_old + _dot_tn(vw, kp)
        n_ref[0, h:h + 1, :] = decay * n_old + jnp.sum(k32 * w_state, axis=0, keepdims=True)
        m_ref[0, h:h + 1, :] = jnp.broadcast_to(m_new, (1, LANES))
        ym_ref[0, :, sl] = _rms(o_ref[0, :, sl] * hid, gout_ref[:, sl]).astype(BF16)


def _mlstm_sample(mq, mk, mv, mo, gates, g_out_m, c0, n0, m0):
    nb = mq.shape[0]
    seq = lambda w: pl.BlockSpec((1, SAMPLE_ROWS, w), lambda b: (b, 0, 0))
    st_c = pl.BlockSpec((1, H_M, D_HM, D_HM), lambda b: (b, 0, 0, 0))
    st_v = pl.BlockSpec((1, H_M, D_HM), lambda b: (b, 0, 0))
    return pl.pallas_call(
        _mlstm_sample_kernel,
        grid=(nb,),
        in_specs=[seq(W_M), seq(W_M), seq(W_M), seq(W_M), seq(GATE_COLS),
                  pl.BlockSpec((1, W_M), lambda b: (0, 0)), st_c, st_v, st_v],
        out_specs=[seq(W_M), st_c, st_v, st_v],
        out_shape=(
            jax.ShapeDtypeStruct((nb, SAMPLE_ROWS, W_M), BF16),
            jax.ShapeDtypeStruct((nb, H_M, D_HM, D_HM), F32),
            jax.ShapeDtypeStruct((nb, H_M, D_HM), F32), jax.ShapeDtypeStruct((nb, H_M, D_HM), F32),
        ),
        compiler_params=_params("arbitrary"),
        name="mlstm_sample",
    )(mq, mk, mv, mo, gates, g_out_m, c0, n0, m0)


def _fox_sample_kernel(pt_ref, q_ref, knew_ref, vnew_ref, lfnew_ref, gout_ref, *rest, pages, n_steps):
    k_refs = rest[:pages]
    v_refs = rest[pages:2 * pages]
    lf_refs = rest[2 * pages:3 * pages]
    y_ref, m_scr, l_scr, acc_scr, carry_scr = rest[3 * pages:]
    step = pl.program_id(1)
    r = SAMPLE_ROWS

    @pl.when(step == 0)
    def _init():
        m_scr[...] = jnp.full_like(m_scr, -jnp.inf)
        l_scr[...] = jnp.zeros_like(l_scr)
        acc_scr[...] = jnp.zeros_like(acc_scr)
        carry_scr[...] = jnp.zeros_like(carry_scr)

    def absorb(scores, pv):
        s = jnp.concatenate(scores, axis=0)
        m_prev = m_scr[...]
        m_new = jnp.maximum(m_prev, jnp.max(s, axis=-1, keepdims=True))
        alpha = jnp.exp2(m_prev - m_new)
        prob = jnp.exp2(s - m_new)
        l_scr[...] = alpha * l_scr[...] + jnp.sum(prob, axis=-1, keepdims=True)
        pb = prob.astype(BF16)
        acc_scr[...] = alpha * acc_scr[...] + jnp.concatenate(
            [pv(h, pb[h * r:(h + 1) * r, :]) for h in range(H_F)], axis=0)
        m_scr[...] = m_new

    upper = (lax.broadcasted_iota(jnp.int32, (PAGE_SIZE, PAGE_SIZE), 0)
             <= lax.broadcasted_iota(jnp.int32, (PAGE_SIZE, PAGE_SIZE), 1)).astype(F32)
    cum = _dot_hi(jnp.concatenate([lf_refs[j][...] for j in range(pages)], axis=0), upper)
    carry = carry_scr[...]
    f_past = []
    for j in range(pages):
        f_past.append(carry + cum[j * H_F:(j + 1) * H_F, :])
        carry = f_past[-1][:, PAGE_SIZE - 1:PAGE_SIZE]
    carry_scr[...] = carry
    heads = [slice(h * D_HF, (h + 1) * D_HF) for h in range(H_F)]
    scores, v_t = [], []
    for h in range(H_F):
        k_t = jnp.concatenate([k_refs[j][h] for j in range(pages)], axis=1).astype(BF16)
        v_t.append(jnp.concatenate([v_refs[j][h] for j in range(pages)], axis=1).astype(BF16))
        bias = jnp.concatenate([f_past[j][h:h + 1, :] for j in range(pages)], axis=1) * LOG2E
        scores.append(_dot(q_ref[0, :, heads[h]], k_t) - bias)
    absorb(scores, lambda h, prob: _dot_nt(prob, v_t[h]))

    @pl.when(step == n_steps - 1)
    def _finish():
        cn = _dot_hi(lfnew_ref[0], upper)[:, :r]
        causal = lax.broadcasted_iota(jnp.int32, (r, r), 1) <= lax.broadcasted_iota(jnp.int32, (r, r), 0)
        scores = []
        for h in range(H_F):
            bias = (cn[h:h + 1, :] + carry[h:h + 1, :]) * LOG2E
            scores.append(jnp.where(causal, _dot_nt(q_ref[0, :, heads[h]], knew_ref[0, :, heads[h]]) - bias, -jnp.inf))
        absorb(scores, lambda h, prob: _dot(prob, vnew_ref[0, :, heads[h]]))
        out = acc_scr[...] / l_scr[...]
        y_ref[0] = jnp.concatenate(
            [_rms(out[h * r:(h + 1) * r, :], gout_ref[:, heads[h]]) for h in range(H_F)], axis=1).astype(BF16)


def _fox_sample(page_table, fq, fk16, fv16, lf_new_t, g_out_f, cache_k, cache_v, cache_lf_t, pages):
    nb, n_pages = page_table.shape
    n_steps = n_pages // pages
    seq = lambda w: pl.BlockSpec((1, SAMPLE_ROWS, w), lambda b, s, pt: (b, 0, 0))

    def page_of(j):
        return lambda b, s, pt: pt[b * n_pages + s * pages + j]

    def kv_page(j):
        f = page_of(j)
        return pl.BlockSpec((None, H_F, D_HF, PAGE_SIZE), lambda b, s, pt: (f(b, s, pt), 0, 0, 0))

    def lf_page(j):
        f = page_of(j)
        return pl.BlockSpec((None, H_F, PAGE_SIZE), lambda b, s, pt: (f(b, s, pt), 0, 0))

    grid_spec = pltpu.PrefetchScalarGridSpec(
        num_scalar_prefetch=1,
        grid=(nb, n_steps),
        in_specs=([seq(W_F), seq(W_F), seq(W_F),
                   pl.BlockSpec((1, H_F, PAGE_SIZE), lambda b, s, pt: (b, 0, 0)),
                   pl.BlockSpec((1, W_F), lambda b, s, pt: (0, 0))]
                  + [kv_page(j) for j in range(pages)] + [kv_page(j) for j in range(pages)]
                  + [lf_page(j) for j in range(pages)]),
        out_specs=seq(W_F),
        scratch_shapes=[pltpu.VMEM((H_F * SAMPLE_ROWS, 1), F32), pltpu.VMEM((H_F * SAMPLE_ROWS, 1), F32),
                        pltpu.VMEM((H_F * SAMPLE_ROWS, D_HF), F32), pltpu.VMEM((H_F, 1), F32)],
    )
    return pl.pallas_call(
        functools.partial(_fox_sample_kernel, pages=pages, n_steps=n_steps),
        grid_spec=grid_spec,
        out_shape=jax.ShapeDtypeStruct((nb, SAMPLE_ROWS, W_F), BF16),
        compiler_params=_params("arbitrary", "arbitrary"),
        name="fox_sample",
    )(page_table.reshape(-1), fq, fk16, fv16, lf_new_t, g_out_f,
      *([cache_k] * pages), *([cache_v] * pages), *([cache_lf_t] * pages))


def _post_kernel(x_ref, ym_ref, yf_ref, gt1_ref, sh2_ref, sc2_ref, gt2_ref, g2_ref, gfin_ref,
                 wom_ref, wof_ref, wr_ref, br_ref, wg_ref, wu_ref, wd_ref,
                 o_ref, xp_scr, hb_scr, comb_scr, acc_scr):
    e = pl.program_id(1)

    @pl.when(e == 0)
    def _prepare():
        mix = _dot(ym_ref[...], wom_ref[...]) + _dot(yf_ref[...], wof_ref[...])
        xp = x_ref[...] + gt1_ref[...] * mix
        xp_scr[...] = xp
        h = _rms(xp, g2_ref[...]) * (1.0 + sc2_ref[...]) + sh2_ref[...]
        hb, h_lo = _split_bf16(h)
        hb_scr[...] = hb
        rr = _dot(hb, wr_ref[...])
        logits = rr[:, :LANES] + rr[:, LANES:] + _dot(h_lo, wr_ref[:, :LANES]) + br_ref[...]
        lane = lax.broadcasted_iota(jnp.int32, logits.shape, 1)
        big = jnp.int32(4 * LANES)
        gl = jnp.where(lane < N_GROUPS, logits, -jnp.inf)
        gmax = jnp.max(gl, axis=-1, keepdims=True)
        gidx = jnp.min(jnp.where(gl == gmax, lane, big), axis=-1, keepdims=True)
        g_w = 1.0 / jnp.sum(jnp.exp(gl - gmax), axis=-1, keepdims=True)
        first = N_GROUPS + EXP_PER_GROUP * gidx
        el = jnp.where((lane >= first) & (lane < first + EXP_PER_GROUP), logits, -jnp.inf)
        emax = jnp.max(el, axis=-1, keepdims=True)
        pe = jnp.exp(el - emax)
        prob = pe / jnp.sum(pe, axis=-1, keepdims=True)
        v1 = jnp.max(prob, axis=-1, keepdims=True)
        i1 = jnp.min(jnp.where(prob == v1, lane, big), axis=-1, keepdims=True)
        rest = jnp.where((lane == i1) | (el == -jnp.inf), -1.0, prob)
        v2 = jnp.max(rest, axis=-1, keepdims=True)
        i2 = jnp.min(jnp.where(rest == v2, lane, big), axis=-1, keepdims=True)
        tot = v1 + v2
        comb_scr[...] = jnp.where(lane == i1, v1 / tot, jnp.where(lane == i2, v2 / tot, 0.0)) * g_w
        acc_scr[...] = jnp.zeros_like(acc_scr)

    hb = hb_scr[...]
    a = _dot(hb, wg_ref[0])
    u = _dot(hb, wu_ref[0])
    lane = lax.broadcasted_iota(jnp.int32, comb_scr.shape, 1)
    ce = jnp.sum(jnp.where(lane == e + N_GROUPS, comb_scr[...], 0.0), axis=-1, keepdims=True)
    act = (a * jax.nn.sigmoid(a) * u * ce).astype(BF16)
    acc_scr[...] += _dot(act, wd_ref[0])

    @pl.when(e == N_EXPERTS - 1)
    def _finish():
        o_ref[...] = _rms(xp_scr[...] + gt2_ref[...] * acc_scr[...], gfin_ref[...])


def _post(x, ym, yf, gt1, sh2, sc2, gt2, g_norm2, g_final, wo_m, wo_f, w_route2, b_route, wg, wu, wd, tm):
    t = x.shape[0]
    mod_rows = gt1.shape[0]
    mod_map = (lambda i, e: (0, 0)) if mod_rows == 1 else (lambda i, e: (i, 0))
    mod = pl.BlockSpec((1, D_MODEL) if mod_rows == 1 else (tm, D_MODEL), mod_map)
    tok = lambda w: pl.BlockSpec((tm, w), lambda i, e: (i, 0))
    const = lambda shape: pl.BlockSpec(shape, lambda i, e: (0,) * len(shape))
    return pl.pallas_call(
        _post_kernel,
        grid=(t // tm, N_EXPERTS),
        in_specs=[
            tok(D_MODEL), tok(W_M), tok(W_F), mod, mod, mod, mod,
            const((1, D_MODEL)), const((1, D_MODEL)),
            const(wo_m.shape), const(wo_f.shape), const(w_route2.shape), const((1, LANES)),
            pl.BlockSpec((1, D_MODEL, D_FF_E), lambda i, e: (e, 0, 0)),
            pl.BlockSpec((1, D_MODEL, D_FF_E), lambda i, e: (e, 0, 0)),
            pl.BlockSpec((1, D_FF_E, D_MODEL), lambda i, e: (e, 0, 0)),
        ],
        out_specs=tok(D_MODEL),
        out_shape=jax.ShapeDtypeStruct((t, D_MODEL), F32),
        scratch_shapes=[pltpu.VMEM((tm, D_MODEL), F32), pltpu.VMEM((tm, D_MODEL), BF16),
                        pltpu.VMEM((tm, LANES), F32), pltpu.VMEM((tm, D_MODEL), F32)],
        compiler_params=_params("arbitrary", "arbitrary"),
        name="post",
    )(x, ym, yf, gt1, sh2, sc2, gt2, g_norm2, g_final, wo_m, wo_f, w_route2, b_route, wg, wu, wd)


def _pad_cols(w, n):
    return jnp.pad(w, ((0, 0), (0, n - w.shape[1])))


def _hi_lo_cols(w):
    hi, lo = _split_bf16(w)
    return jnp.concatenate([hi, lo], axis=1)


def _pad_seq(a):
    nb = a.shape[0] // DEC_SEQ
    return jnp.pad(a.reshape(nb, DEC_SEQ, a.shape[1]), ((0, 0), (0, SAMPLE_ROWS - DEC_SEQ), (0, 0)))


def _unpad_seq(a):
    return a[:, :DEC_SEQ].reshape(a.shape[0] * DEC_SEQ, a.shape[2])


def kernel(x_prompt, x_sample, c_prompt, c_sample, cache_k, cache_v, cache_logf, state_C, state_n, state_m,
           page_table, w_ada, b_ada, g_norm1, g_norm2, w_in, b_ig, b_fg, b_ff, g_out_m, g_out_f, w_out,
           w_group, b_group, w_router, b_router, w_gate, w_up, w_down, g_final):
    depth = w_ada.shape[0]
    assert depth == 1 and x_prompt.shape[0] == 1
    t = x_prompt.shape[1]
    nb, ds = x_sample.shape[:2]
    assert ds == DEC_SEQ
    ts = nb * ds
    xp = x_prompt.reshape(t, D_MODEL)
    xs = x_sample.reshape(ts, D_MODEL)
    l = 0

    w = w_in[l]
    o = np.cumsum([0, W_M, W_M, W_M, W_M, H_M, H_M, W_F, W_F, W_F, H_F])
    w_main = jnp.concatenate([w[:, o[0]:o[4]], w[:, o[6]:o[9]]], axis=1).astype(BF16)
    w_gates = _pad_cols(jnp.concatenate([w[:, o[4]:o[6]], w[:, o[9]:o[10]]], axis=1), GATE_COLS)
    w_gate2 = _hi_lo_cols(w_gates)
    gate_bias = _pad_cols(jnp.concatenate([b_ig[l], b_fg[l], b_ff[l]])[None, :], GATE_COLS)
    gm = g_out_m[l].reshape(1, W_M)
    gf = g_out_f[l].reshape(1, W_F)
    wo_m = w_out[l][:W_M].astype(BF16)
    wo_f = w_out[l][W_M:].astype(BF16)
    w_route = jnp.concatenate(
        [w_group[l], jnp.transpose(w_router[l], (1, 0, 2)).reshape(D_MODEL, N_EXPERTS)], axis=1)
    w_route2 = _hi_lo_cols(_pad_cols(w_route, LANES))
    b_route = _pad_cols(jnp.concatenate([b_group[l], b_router[l].reshape(-1)])[None, :], LANES)
    wg = w_gate[l].reshape(N_EXPERTS, D_MODEL, D_FF_E).astype(BF16)
    wu = w_up[l].reshape(N_EXPERTS, D_MODEL, D_FF_E).astype(BF16)
    wd = w_down[l].reshape(N_EXPERTS, D_FF_E, D_MODEL).astype(BF16)

    rows = -(-(1 + nb) // 8) * 8
    c_all = jnp.pad(jnp.concatenate([c_prompt, c_sample], axis=0), ((0, rows - 1 - nb), (0, 0)))
    mod = _adaln(c_all, w_ada[l], b_ada[l])
    p_mod = [mod[0:1, i * D_MODEL:(i + 1) * D_MODEL] for i in range(6)]
    s_mod = [jnp.repeat(mod[1:1 + nb, i * D_MODEL:(i + 1) * D_MODEL], ds, axis=0) for i in range(6)]
    g1 = g_norm1[l][None, :]
    g2 = g_norm2[l][None, :]
    gfin = g_final[None, :]

    tm = min(TOKEN_TILE, t)
    tq = min(FOX_BLOCK, t)
    mq, mk, mv, mo, fq, fk, fv, fk16, fv16, gates = _inproj(xp, p_mod[0], p_mod[1], g1, w_main, w_gate2, gate_bias, tm)
    ym, fcol, c_p, nm_p = _mlstm_prompt(mq, mk, mv, mo, gates, gm)
    qat, ka, vat, stats = _fox_prep(fq, fk16, fv16, fcol, tq)
    yf = _fox_prompt(qat, ka, vat, stats, gf.reshape(W_F, 1), tq)
    y_prompt = _post(xp, ym, yf, p_mod[2], p_mod[3], p_mod[4], p_mod[5], g2, gfin,
                     wo_m, wo_f, w_route2, b_route, wg, wu, wd, min(POST_TILE, t))

    smq, smk, smv, smo, sfq, sfk, sfv, sfk16, sfv16, sgates = _inproj(
        xs, s_mod[0], s_mod[1], g1, w_main, w_gate2, gate_bias, ts)
    m0 = jnp.broadcast_to(state_m[l][:, :, None], (nb, H_M, LANES))
    sym, c_s, n_s, m_s = _mlstm_sample(_pad_seq(smq), _pad_seq(smk), _pad_seq(smv), _pad_seq(smo), _pad_seq(sgates),
                                       gm, state_C[l], state_n[l], m0)
    lf_new = sgates[:, FOX_COL:FOX_COL + H_F]
    lf_new_t = jnp.pad(jnp.transpose(lf_new.reshape(nb, ds, H_F), (0, 2, 1)), ((0, 0), (0, 0), (0, PAGE_SIZE - ds)))
    syf = _fox_sample(page_table, _pad_seq(sfq), _pad_seq(sfk16), _pad_seq(sfv16), lf_new_t, gf,
                      jnp.transpose(cache_k[l], (0, 2, 3, 1)), jnp.transpose(cache_v[l], (0, 2, 3, 1)),
                      jnp.transpose(cache_logf[l], (0, 2, 1)), pages=FOX_SAMPLE_PAGES)
    y_sample = _post(xs, _unpad_seq(sym), _unpad_seq(syf), s_mod[2], s_mod[3], s_mod[4], s_mod[5], g2, gfin,
                     wo_m, wo_f, w_route2, b_route, wg, wu, wd, ts)

    return (
        y_prompt.reshape(1, t, D_MODEL), y_sample.reshape(nb, ds, D_MODEL),
        fk.reshape(1, 1, t, H_F, D_HF), fv.reshape(1, 1, t, H_F, D_HF),
        gates[:, FOX_COL:FOX_COL + H_F].reshape(1, 1, t, H_F),
        c_p[None, None], nm_p[None, None, :H_M], nm_p[None, None, H_M:, 0],
        sfk.reshape(1, nb, ds, H_F, D_HF), sfv.reshape(1, nb, ds, H_F, D_HF), lf_new.reshape(1, nb, ds, H_F),
        c_s[None], n_s[None], m_s[None, :, :, 0],
    )
```

```python
import functools
import math

import jax
import jax.numpy as jnp
import numpy as np
from jax import lax
from jax.experimental import pallas as pl
from jax.experimental.pallas import tpu as pltpu

F32 = jnp.float32
BF16 = jnp.bfloat16
HIGHEST = lax.Precision.HIGHEST

D_MODEL = 1024
H_M = 4
D_HM = 128
W_M = H_M * D_HM
H_F = 8
D_HF = 64
W_F = H_F * D_HF
PAGE_SIZE = 128
CHUNK = 128
DEC_SEQ = 4
N_GROUPS = 4
EXP_PER_GROUP = 4
N_EXPERTS = N_GROUPS * EXP_PER_GROUP
D_FF_E = D_MODEL // 4
FOX_SCALE = D_HF ** -0.5
LOG2E = math.log2(math.e)
EPS = 1e-6

LANES = 128
BF16_SUBLANES = 16
MXU_DIM = 256
GATE_COLS = LANES
FOX_COL = 2 * H_M
SAMPLE_ROWS = BF16_SUBLANES
AUGW = MXU_DIM
VROWS = D_HF + BF16_SUBLANES
TOKEN_TILE = 512
POST_TILE = 1024
FOX_BLOCK = 1024
STATS_ROWS = 16
SKIP_BELOW = 160.0
VMEM_LIMIT = 60000 * 1024


def _params(*sem):
    return pltpu.CompilerParams(dimension_semantics=sem, vmem_limit_bytes=VMEM_LIMIT)


def _dot(a, b):
    return jnp.dot(a, b, preferred_element_type=F32)


def _dot_nt(a, b):
    return lax.dot_general(a, b, (((1,), (1,)), ((), ())), preferred_element_type=F32)


def _dot_tn(a, b):
    return lax.dot_general(a, b, (((0,), (0,)), ((), ())), preferred_element_type=F32)


def _dot_hi(a, b):
    return jnp.dot(a, b, preferred_element_type=F32, precision=HIGHEST)


def _rms(x, g):
    return x * lax.rsqrt(jnp.mean(x * x, axis=-1, keepdims=True) + EPS) * g


def _log_sigmoid(x):
    return jnp.minimum(x, 0.0) - jnp.log1p(jnp.exp(-jnp.abs(x)))


def _split_bf16(x):
    hi = x.astype(BF16)
    lo = (x - hi.astype(F32)).astype(BF16)
    return hi, lo


def _split3(x):
    hi = x.astype(BF16).astype(F32)
    mid = (x - hi).astype(BF16).astype(F32)
    lo = (x - hi - mid).astype(BF16).astype(F32)
    return hi, mid, lo


def _adaln_kernel(c_ref, w_ref, b_ref, o_ref):
    c = c_ref[...]
    o_ref[...] = _dot_hi(c * jax.nn.sigmoid(c), w_ref[...]) + b_ref[...]


def _adaln(c_all, w_ada, b_ada):
    rows = c_all.shape[0]
    n = w_ada.shape[1]
    tn = 1536
    return pl.pallas_call(
        _adaln_kernel,
        grid=(n // tn,),
        in_specs=[
            pl.BlockSpec((rows, D_MODEL), lambda j: (0, 0)),
            pl.BlockSpec((D_MODEL, tn), lambda j: (0, j)),
            pl.BlockSpec((1, tn), lambda j: (0, j)),
        ],
        out_specs=pl.BlockSpec((rows, tn), lambda j: (0, j)),
        out_shape=jax.ShapeDtypeStruct((rows, n), F32),
        compiler_params=_params("arbitrary"),
        name="adaln",
    )(c_all, w_ada, b_ada.reshape(1, n))


def _inproj_kernel(x_ref, shift_ref, scale_ref, g_ref, w_ref, wg_ref, gb_ref,
                   mq_ref, mk_ref, mv_ref, mo_ref, fq_ref, fk_ref, fv_ref, fk16_ref, fv16_ref, gate_ref):
    h = _rms(x_ref[...], g_ref[...]) * (1.0 + scale_ref[...]) + shift_ref[...]
    hb, h_lo = _split_bf16(h)

    def proj(j):
        return _dot(hb, w_ref[:, j * W_M:(j + 1) * W_M])

    mq_ref[...] = proj(0).astype(BF16)
    mk_ref[...] = (proj(1) * (D_HM ** -0.5)).astype(BF16)
    mv_ref[...] = proj(2).astype(BF16)
    mo_ref[...] = jax.nn.sigmoid(proj(3))
    fq_ref[...] = (proj(4) * (FOX_SCALE * LOG2E)).astype(BF16)
    fk = proj(5)
    fk_ref[...] = fk
    fk16_ref[...] = fk.astype(BF16)
    fv = proj(6)
    fv_ref[...] = fv
    fv16_ref[...] = fv.astype(BF16)
    gg = _dot(hb, wg_ref[...])
    gpre = gg[:, :GATE_COLS] + gg[:, GATE_COLS:] + _dot(h_lo, wg_ref[:, :GATE_COLS]) + gb_ref[...]
    lane = lax.broadcasted_iota(jnp.int32, gpre.shape, 1)
    gate_ref[...] = jnp.where(lane < H_M, gpre, jnp.where(lane < FOX_COL + H_F, _log_sigmoid(gpre), 0.0))


def _inproj(x, shift, scale, g_norm, w_main, w_gate2, gate_bias, tm):
    t = x.shape[0]
    mod_rows = shift.shape[0]
    mod_map = (lambda i: (0, 0)) if mod_rows == 1 else (lambda i: (i, 0))
    mod_block = (1, D_MODEL) if mod_rows == 1 else (tm, D_MODEL)
    tok = lambda w: pl.BlockSpec((tm, w), lambda i: (i, 0))
    const = lambda shape: pl.BlockSpec(shape, lambda i: (0, 0))
    out_shapes = (
        jax.ShapeDtypeStruct((t, W_M), BF16), jax.ShapeDtypeStruct((t, W_M), BF16),
        jax.ShapeDtypeStruct((t, W_M), BF16), jax.ShapeDtypeStruct((t, W_M), F32),
        jax.ShapeDtypeStruct((t, W_F), BF16), jax.ShapeDtypeStruct((t, W_F), F32),
        jax.ShapeDtypeStruct((t, W_F), F32), jax.ShapeDtypeStruct((t, W_F), BF16),
        jax.ShapeDtypeStruct((t, W_F), BF16), jax.ShapeDtypeStruct((t, GATE_COLS), F32),
    )
    return pl.pallas_call(
        _inproj_kernel,
        grid=(t // tm,),
        in_specs=[
            tok(D_MODEL),
            pl.BlockSpec(mod_block, mod_map), pl.BlockSpec(mod_block, mod_map),
            const((1, D_MODEL)), const(w_main.shape), const(w_gate2.shape), const((1, GATE_COLS)),
        ],
        out_specs=[tok(W_M)] * 4 + [tok(W_F)] * 5 + [tok(GATE_COLS)],
        out_shape=out_shapes,
        compiler_params=_params("arbitrary"),
        name="inproj",
    )(x, shift, scale, g_norm, w_main, w_gate2, gate_bias)


def _mlstm_prompt_kernel(q_ref, k_ref, v_ref, o_ref, g_ref, gout_ref,
                         ym_ref, fcol_ref, c_ref, nm_ref, fcarry_ref):
    L = CHUNK

    @pl.when(pl.program_id(0) == 0)
    def _init():
        c_ref[...] = jnp.zeros_like(c_ref)
        nm_ref[...] = jnp.zeros_like(nm_ref)
        fcarry_ref[...] = jnp.zeros_like(fcarry_ref)

    row = lax.broadcasted_iota(jnp.int32, (L, L), 0)
    col = lax.broadcasted_iota(jnp.int32, (L, L), 1)
    causal = row >= col
    gates = g_ref[...]
    cum = _dot_hi(causal.astype(F32), gates)
    gates_t = gates.T
    cum_t = cum.T
    for h in range(H_M):
        sl = slice(h * D_HM, (h + 1) * D_HM)
        a_row = gates_t[h:h + 1, :] - cum_t[H_M + h:H_M + h + 1, :]
        b_col = cum[:, H_M + h:H_M + h + 1]
        a_col = gates[:, h:h + 1] - b_col
        m_prev = nm_ref[H_M + h:H_M + h + 1, 0:1]
        a_mat = jnp.where(causal, a_row, -jnp.inf)
        r = jnp.maximum(jnp.max(a_mat, axis=-1, keepdims=True), m_prev)
        w_intra = jnp.exp(a_mat - r)
        w_inter = jnp.exp(m_prev - r)
        q = q_ref[:, sl]
        k = k_ref[:, sl]
        v = v_ref[:, sl]
        s = _dot_nt(q, k) * w_intra
        c_old = c_ref[h]
        n_old = nm_ref[h:h + 1, :]
        num = _dot(s.astype(BF16), v) + w_inter * _dot_nt(q, c_old.astype(BF16))
        den = (jnp.sum(s, axis=-1, keepdims=True)
               + w_inter * jnp.sum(q.astype(F32) * n_old, axis=-1, keepdims=True))
        m_t = b_col + r
        hid = num / jnp.maximum(jnp.abs(den), jnp.exp(-m_t))
        m_new = m_t[L - 1:L, :]
        b_last = b_col[L - 1:L, :]
        w_state = jnp.exp(a_col + (b_last - m_new))
        decay = jnp.exp(b_last + m_prev - m_new)
        vw = (v.astype(F32) * w_state).astype(BF16)
        c_ref[h] = decay * c_old + _dot_tn(vw, k)
        nm_ref[h:h + 1, :] = decay * n_old + jnp.sum(k.astype(F32) * w_state, axis=0, keepdims=True)
        nm_ref[H_M + h:H_M + h + 1, :] = jnp.broadcast_to(m_new, (1, LANES))
        ym_ref[:, sl] = _rms(o_ref[:, sl] * hid, gout_ref[:, sl]).astype(BF16)
    f_glob = cum + fcarry_ref[...]
    fcol_ref[...] = f_glob
    fcarry_ref[...] = f_glob[L - 1:L, :]


def _mlstm_prompt(mq, mk, mv, mo, gates, g_out_m):
    t = mq.shape[0]
    tok = lambda w: pl.BlockSpec((CHUNK, w), lambda c: (c, 0))
    return pl.pallas_call(
        _mlstm_prompt_kernel,
        grid=(t // CHUNK,),
        in_specs=[tok(W_M), tok(W_M), tok(W_M), tok(W_M), tok(GATE_COLS),
                  pl.BlockSpec((1, W_M), lambda c: (0, 0))],
        out_specs=[
            tok(W_M), tok(GATE_COLS),
            pl.BlockSpec((H_M, D_HM, D_HM), lambda c: (0, 0, 0)),
            pl.BlockSpec((2 * H_M, D_HM), lambda c: (0, 0)),
        ],
        out_shape=(
            jax.ShapeDtypeStruct((t, W_M), BF16), jax.ShapeDtypeStruct((t, GATE_COLS), F32),
            jax.ShapeDtypeStruct((H_M, D_HM, D_HM), F32), jax.ShapeDtypeStruct((2 * H_M, D_HM), F32),
        ),
        scratch_shapes=[pltpu.VMEM((1, GATE_COLS), F32)],
        compiler_params=_params("arbitrary"),
        name="mlstm_prompt",
    )(mq, mk, mv, mo, gates, g_out_m)


def _fox_prep_kernel(q_ref, k_ref, v_ref, fcol_ref, qat_ref, ka_ref, vat_ref, stats_ref):
    tm = q_ref.shape[0]
    lane = lax.broadcasted_iota(jnp.int32, (tm, LANES), 1)
    lo = lane < D_HF
    f_all = fcol_ref[...] * LOG2E
    ones_rows = jnp.where(lax.broadcasted_iota(jnp.int32, (VROWS - D_HF, tm), 0) == 0, 1.0, 0.0).astype(BF16)
    half_of_lane = lax.broadcasted_iota(jnp.int32, (LANES, LANES), 0) // D_HF
    half_sum = (half_of_lane == lax.broadcasted_iota(jnp.int32, (LANES, LANES), 1)).astype(BF16)
    q_norms, k_norms = [], []
    for p in range(H_F // 2):
        sl = slice(p * LANES, (p + 1) * LANES)
        q2 = q_ref[:, sl].astype(F32)
        k2 = k_ref[:, sl].astype(F32)
        q_norms.append(jnp.max(_dot((q2 * q2).astype(BF16), half_sum), axis=0, keepdims=True))
        k_norms.append(jnp.max(_dot((k2 * k2).astype(BF16), half_sum), axis=0, keepdims=True))
        k_aug = jnp.zeros((tm, LANES), F32)
        for half in range(2):
            h = 2 * p + half
            hi, mid, lw = _split3(f_all[:, FOX_COL + h:FOX_COL + h + 1])
            b = 8 * half
            q_aug = jnp.where(lane == b, hi, jnp.where(lane == b + 1, mid, jnp.where(lane == b + 2, lw, jnp.where(
                (lane >= b + 3) & (lane < b + 6), 1.0, 0.0))))
            k_aug = k_aug + jnp.where((lane >= b) & (lane < b + 3), 1.0, jnp.where(
                lane == b + 3, -hi, jnp.where(lane == b + 4, -mid, jnp.where(lane == b + 5, -lw, 0.0))))
            q_own = jnp.where(lo, q2, 0.0) if half == 0 else jnp.where(lo, 0.0, q2)
            qat_ref[h * AUGW:h * AUGW + LANES, :] = q_own.T.astype(BF16)
            qat_ref[h * AUGW + LANES:(h + 1) * AUGW, :] = q_aug.T.astype(BF16)
        ka_ref[:, p * AUGW:p * AUGW + LANES] = k_ref[:, sl]
        ka_ref[:, p * AUGW + LANES:(p + 1) * AUGW] = k_aug.astype(BF16)
        vt = v_ref[:, sl].astype(F32).T.astype(BF16)
        for half in range(2):
            h = 2 * p + half
            vat_ref[h * VROWS:h * VROWS + D_HF, :] = vt[half * D_HF:(half + 1) * D_HF, :]
            vat_ref[h * VROWS + D_HF:(h + 1) * VROWS, :] = ones_rows
    pad = jnp.zeros((STATS_ROWS - H_F - 2, LANES), F32)
    stats_ref[0] = jnp.concatenate(q_norms + k_norms + [f_all[0:1, :], f_all[tm - 1:tm, :], pad], axis=0)


def _fox_prep(fq, fk16, fv16, fcol, tm):
    t = fq.shape[0]
    tok = lambda w: pl.BlockSpec((tm, w), lambda i: (i, 0))
    tr = lambda r: pl.BlockSpec((r, tm), lambda i: (0, i))
    return pl.pallas_call(
        _fox_prep_kernel,
        grid=(t // tm,),
        in_specs=[tok(W_F), tok(W_F), tok(W_F), tok(GATE_COLS)],
        out_specs=[tr(H_F * AUGW), tok(H_F // 2 * AUGW), tr(H_F * VROWS),
                   pl.BlockSpec((1, STATS_ROWS, LANES), lambda i: (i, 0, 0))],
        out_shape=(jax.ShapeDtypeStruct((H_F * AUGW, t), BF16), jax.ShapeDtypeStruct((t, H_F // 2 * AUGW), BF16),
                   jax.ShapeDtypeStruct((H_F * VROWS, t), BF16), jax.ShapeDtypeStruct((t // tm, STATS_ROWS, LANES), F32)),
        compiler_params=_params("arbitrary"),
        name="fox_prep",
    )(fq, fk16, fv16, fcol)


def _fox_prompt_kernel(qtab_ref, ktab_ref, live_ref, kfetch_ref, qat_ref, ka_ref, vat_ref, gcol_ref,
                       y_ref, m_scr, acc_scr, *, tq):
    i = pl.program_id(0)
    qb = qtab_ref[i]
    kb = ktab_ref[i]

    @pl.when(kb == 0)
    def _init():
        m_scr[...] = jnp.full_like(m_scr, -jnp.inf)
        acc_scr[...] = jnp.zeros_like(acc_scr)

    def head_update(h, diagonal):
        p = h // 2
        s = _dot(ka_ref[:, p * AUGW:(p + 1) * AUGW], qat_ref[h * AUGW:(h + 1) * AUGW, :])
        if diagonal:
            keep = (lax.broadcasted_iota(jnp.int32, (tq, tq), 0) <= lax.broadcasted_iota(jnp.int32, (tq, tq), 1))
            s = jnp.where(keep, s, -jnp.inf)
        m_prev = m_scr[h]
        m_new = jnp.maximum(m_prev, jnp.max(s, axis=0, keepdims=True))
        m_scr[h] = m_new
        prob_t = jnp.exp2(s - m_new).astype(BF16)
        acc_scr[h] = jnp.exp2(m_prev - m_new) * acc_scr[h] + _dot(vat_ref[h * VROWS:(h + 1) * VROWS, :], prob_t)

    @pl.when(kb < qb)
    def _off_diagonal():
        for h in range(H_F):
            @pl.when(live_ref[i * H_F + h] != 0)
            def _live_head():
                head_update(h, False)

    @pl.when(kb == qb)
    def _diagonal():
        for h in range(H_F):
            head_update(h, True)
        for p in range(H_F // 2):
            outs = []
            for h in (2 * p, 2 * p + 1):
                acc = acc_scr[h]
                out = acc[:D_HF, :] / acc[D_HF:D_HF + 1, :]
                ms = jnp.mean(out * out, axis=0, keepdims=True)
                outs.append(out * lax.rsqrt(ms + EPS) * gcol_ref[h * D_HF:(h + 1) * D_HF, :])
            y_ref[:, p * LANES:(p + 1) * LANES] = jnp.concatenate(outs, axis=0).T.astype(BF16)


def _fox_live(stats, qtab, ktab):
    slack = 1.0 + 2.0 ** -6
    qn = jnp.sqrt(stats[:, 0:H_F // 2, 0:2].reshape(-1, H_F) * slack)
    kn = jnp.sqrt(stats[:, H_F // 2:H_F, 0:2].reshape(-1, H_F) * slack)
    f_first = stats[:, H_F, FOX_COL:FOX_COL + H_F]
    f_last = stats[:, H_F + 1, FOX_COL:FOX_COL + H_F]
    bound = qn[qtab] * (kn[ktab] + kn[qtab]) + f_first[qtab] - f_last[ktab]
    dead = (bound < -SKIP_BELOW) & (ktab < qtab - 1)[:, None]
    return jnp.logical_not(dead).astype(jnp.int32).reshape(-1)


def _fox_prompt(qat, ka, vat, stats, g_col, tq):
    t = ka.shape[0]
    nb = t // tq
    pairs = [(q, k) for q in range(nb) for k in range(q + 1)]
    qtab = np.array([p[0] for p in pairs], np.int32)
    ktab = np.array([p[1] for p in pairs], np.int32)
    live = _fox_live(stats, qtab, ktab)
    fetch = jnp.any(live.reshape(len(pairs), H_F) != 0, axis=1) | (ktab == qtab)
    kfetch = jnp.asarray(ktab)[lax.cummax(jnp.where(fetch, jnp.arange(len(pairs)), 0))]
    grid_spec = pltpu.PrefetchScalarGridSpec(
        num_scalar_prefetch=4,
        grid=(len(pairs),),
        in_specs=[
            pl.BlockSpec((H_F * AUGW, tq), lambda i, qt, kt, lv, kf: (0, qt[i])),
            pl.BlockSpec((tq, H_F // 2 * AUGW), lambda i, qt, kt, lv, kf: (kf[i], 0)),
            pl.BlockSpec((H_F * VROWS, tq), lambda i, qt, kt, lv, kf: (0, kf[i])),
            pl.BlockSpec((W_F, 1), lambda i, qt, kt, lv, kf: (0, 0)),
        ],
        out_specs=pl.BlockSpec((tq, W_F), lambda i, qt, kt, lv, kf: (qt[i], 0)),
        scratch_shapes=[pltpu.VMEM((H_F, 1, tq), F32), pltpu.VMEM((H_F, VROWS, tq), F32)],
    )
    return pl.pallas_call(
        functools.partial(_fox_prompt_kernel, tq=tq),
        grid_spec=grid_spec,
        out_shape=jax.ShapeDtypeStruct((t, W_F), BF16),
        compiler_params=_params("arbitrary"),
        name="fox_prompt",
    )(jnp.asarray(qtab), jnp.asarray(ktab), live, kfetch, qat, ka, vat, g_col)


def _mlstm_sample_kernel(q_ref, k_ref, v_ref, o_ref, g_ref, gout_ref, c0_ref, n0_ref, m0_ref,
                         ym_ref, c_ref, n_ref, m_ref):
    R = SAMPLE_ROWS
    gates = g_ref[0]
    rows = [gates[0:1, :]]
    for t in range(1, DEC_SEQ):
        rows.append(rows[-1] + gates[t:t + 1, :])
    rows += [rows[-1]] * (R - DEC_SEQ)
    cum = jnp.concatenate(rows, axis=0)
    ridx = lax.broadcasted_iota(jnp.int32, (R, 1), 0)
    for h in range(H_M):
        sl = slice(h * D_HM, (h + 1) * D_HM)
        b_col = cum[:, H_M + h:H_M + h + 1]
        a_col = gates[:, h:h + 1] - b_col
        m_prev = m0_ref[0, h:h + 1, 0:1]
        cm = jnp.full((R, 1), -jnp.inf, F32)
        for s in range(DEC_SEQ):
            cm = jnp.maximum(cm, jnp.where(ridx >= s, a_col[s:s + 1, :], -jnp.inf))
        r = jnp.maximum(cm, m_prev)
        w_inter = jnp.exp(m_prev - r)
        q = q_ref[0, :, sl]
        k = k_ref[0, :, sl]
        v = v_ref[0, :, sl]
        q32, k32, v32 = q.astype(F32), k.astype(F32), v.astype(F32)
        c_old = c0_ref[0, h]
        n_old = n0_ref[0, h:h + 1, :]
        num = w_inter * _dot_nt(q, c_old.astype(BF16))
        den = w_inter * jnp.sum(q32 * n_old, axis=-1, keepdims=True)
        for s in range(DEC_SEQ):
            w_s = jnp.where(ridx >= s, jnp.exp(a_col[s:s + 1, :] - r), 0.0)
            s_col = jnp.sum(q32 * k32[s:s + 1, :], axis=-1, keepdims=True) * w_s
            num = num + s_col * v32[s:s + 1, :]
            den = den + s_col
        m_t = b_col + r
        hid = num / jnp.maximum(jnp.abs(den), jnp.exp(-m_t))
        m_new = m_t[DEC_SEQ - 1:DEC_SEQ, :]
        b_last = b_col[DEC_SEQ - 1:DEC_SEQ, :]
        w_state = jnp.where(ridx < DEC_SEQ, jnp.exp(a_col + (b_last - m_new)), 0.0)
        decay = jnp.exp(b_last + m_prev - m_new)
        pad = jnp.zeros((D_HM - R, D_HM), BF16)
        vw = jnp.concatenate([(v32 * w_state).astype(BF16), pad], axis=0)
        kp = jnp.concatenate([k, pad], axis=0)
        c_ref[0, h] = decay * c_old + _dot_tn(vw, kp)
        n_ref[0, h:h + 1, :] = decay * n_old + jnp.sum(k32 * w_state, axis=0, keepdims=True)
        m_ref[0, h:h + 1, :] = jnp.broadcast_to(m_new, (1, LANES))
        ym_ref[0, :, sl] = _rms(o_ref[0, :, sl] * hid, gout_ref[:, sl]).astype(BF16)


def _mlstm_sample(mq, mk, mv, mo, gates, g_out_m, c0, n0, m0):
    nb = mq.shape[0]
    seq = lambda w: pl.BlockSpec((1, SAMPLE_ROWS, w), lambda b: (b, 0, 0))
    st_c = pl.BlockSpec((1, H_M, D_HM, D_HM), lambda b: (b, 0, 0, 0))
    st_v = pl.BlockSpec((1, H_M, D_HM), lambda b: (b, 0, 0))
    return pl.pallas_call(
        _mlstm_sample_kernel,
        grid=(nb,),
        in_specs=[seq(W_M), seq(W_M), seq(W_M), seq(W_M), seq(GATE_COLS),
                  pl.BlockSpec((1, W_M), lambda b: (0, 0)), st_c, st_v, st_v],
        out_specs=[seq(W_M), st_c, st_v, st_v],
        out_shape=(
            jax.ShapeDtypeStruct((nb, SAMPLE_ROWS, W_M), BF16),
            jax.ShapeDtypeStruct((nb, H_M, D_HM, D_HM), F32),
            jax.ShapeDtypeStruct((nb, H_M, D_HM), F32), jax.ShapeDtypeStruct((nb, H_M, D_HM), F32),
        ),
        compiler_params=_params("arbitrary"),
        name="mlstm_sample",
    )(mq, mk, mv, mo, gates, g_out_m, c0, n0, m0)


def _fox_sample_step(step, n_steps, q_ref, knew_ref, vnew_ref, lfnew_ref, gout_ref, k_refs, v_refs, lf_refs,
                     y_ref, m_scr, l_scr, acc_scr, carry_scr, alongside):
    pages = len(k_refs)
    r = SAMPLE_ROWS

    @pl.when(step == 0)
    def _init():
        m_scr[...] = jnp.full_like(m_scr, -jnp.inf)
        l_scr[...] = jnp.zeros_like(l_scr)
        acc_scr[...] = jnp.zeros_like(acc_scr)
        carry_scr[...] = jnp.zeros_like(carry_scr)

    def absorb(scores, pv):
        s = jnp.concatenate(scores, axis=0)
        m_prev = m_scr[...]
        m_new = jnp.maximum(m_prev, jnp.max(s, axis=-1, keepdims=True))
        alpha = jnp.exp2(m_prev - m_new)
        prob = jnp.exp2(s - m_new)
        l_scr[...] = alpha * l_scr[...] + jnp.sum(prob, axis=-1, keepdims=True)
        pb = prob.astype(BF16)
        acc_scr[...] = alpha * acc_scr[...] + jnp.concatenate(
            [pv(h, pb[h * r:(h + 1) * r, :]) for h in range(H_F)], axis=0)
        m_scr[...] = m_new

    upper = (lax.broadcasted_iota(jnp.int32, (PAGE_SIZE, PAGE_SIZE), 0)
             <= lax.broadcasted_iota(jnp.int32, (PAGE_SIZE, PAGE_SIZE), 1)).astype(F32)
    cum = _dot_hi(jnp.concatenate([lf_refs[j][...] for j in range(pages)], axis=0), upper)
    carry = carry_scr[...]
    f_past = []
    for j in range(pages):
        f_past.append(carry + cum[j * H_F:(j + 1) * H_F, :])
        carry = f_past[-1][:, PAGE_SIZE - 1:PAGE_SIZE]
    carry_scr[...] = carry
    heads = [slice(h * D_HF, (h + 1) * D_HF) for h in range(H_F)]
    scores, v_t = [], []
    for h in range(H_F):
        k_t = jnp.concatenate([k_refs[j][h] for j in range(pages)], axis=1).astype(BF16)
        v_t.append(jnp.concatenate([v_refs[j][h] for j in range(pages)], axis=1).astype(BF16))
        bias = jnp.concatenate([f_past[j][h:h + 1, :] for j in range(pages)], axis=1) * LOG2E
        scores.append(_dot(q_ref[0, :, heads[h]], k_t) - bias)
    alongside()
    absorb(scores, lambda h, prob: _dot_nt(prob, v_t[h]))

    @pl.when(step == n_steps - 1)
    def _finish():
        cn = _dot_hi(lfnew_ref[0], upper)[:, :r]
        causal = lax.broadcasted_iota(jnp.int32, (r, r), 1) <= lax.broadcasted_iota(jnp.int32, (r, r), 0)
        scores = []
        for h in range(H_F):
            bias = (cn[h:h + 1, :] + carry[h:h + 1, :]) * LOG2E
            scores.append(jnp.where(causal, _dot_nt(q_ref[0, :, heads[h]], knew_ref[0, :, heads[h]]) - bias, -jnp.inf))
        absorb(scores, lambda h, prob: _dot(prob, vnew_ref[0, :, heads[h]]))
        out = acc_scr[...] / l_scr[...]
        y_ref[0] = jnp.concatenate(
            [_rms(out[h * r:(h + 1) * r, :], gout_ref[:, heads[h]]) for h in range(H_F)], axis=1).astype(BF16)


N_POST_IN = 16
N_FOX_FIXED_IN = 5


def _post_kernel(pt_ref, *refs, fox_pages, fox_steps):
    (x_ref, ym_ref, yf_ref, gt1_ref, sh2_ref, sc2_ref, gt2_ref, g2_ref, gfin_ref,
     wom_ref, wof_ref, wr_ref, br_ref, wg_ref, wu_ref, wd_ref) = refs[:N_POST_IN]
    n_fox_in = N_FOX_FIXED_IN + 3 * fox_pages if fox_pages else 0
    fox_in = refs[N_POST_IN:N_POST_IN + n_fox_in]
    outs = refs[N_POST_IN + n_fox_in:]
    if fox_pages:
        o_ref, y_ref, xp_scr, hb_scr, comb_scr, acc_scr = outs[:6]
        fox_scr = outs[6:]
    else:
        o_ref, xp_scr, hb_scr, comb_scr, acc_scr = outs
    e = pl.program_id(1)

    @pl.when(e == 0)
    def _prepare():
        mix = _dot(ym_ref[...], wom_ref[...]) + _dot(yf_ref[...], wof_ref[...])
        xp = x_ref[...] + gt1_ref[...] * mix
        xp_scr[...] = xp
        h = _rms(xp, g2_ref[...]) * (1.0 + sc2_ref[...]) + sh2_ref[...]
        hb, h_lo = _split_bf16(h)
        hb_scr[...] = hb
        rr = _dot(hb, wr_ref[...])
        logits = rr[:, :LANES] + rr[:, LANES:] + _dot(h_lo, wr_ref[:, :LANES]) + br_ref[...]
        lane = lax.broadcasted_iota(jnp.int32, logits.shape, 1)
        big = jnp.int32(4 * LANES)
        gl = jnp.where(lane < N_GROUPS, logits, -jnp.inf)
        gmax = jnp.max(gl, axis=-1, keepdims=True)
        gidx = jnp.min(jnp.where(gl == gmax, lane, big), axis=-1, keepdims=True)
        g_w = 1.0 / jnp.sum(jnp.exp(gl - gmax), axis=-1, keepdims=True)
        first = N_GROUPS + EXP_PER_GROUP * gidx
        el = jnp.where((lane >= first) & (lane < first + EXP_PER_GROUP), logits, -jnp.inf)
        emax = jnp.max(el, axis=-1, keepdims=True)
        pe = jnp.exp(el - emax)
        prob = pe / jnp.sum(pe, axis=-1, keepdims=True)
        v1 = jnp.max(prob, axis=-1, keepdims=True)
        i1 = jnp.min(jnp.where(prob == v1, lane, big), axis=-1, keepdims=True)
        rest = jnp.where((lane == i1) | (el == -jnp.inf), -1.0, prob)
        v2 = jnp.max(rest, axis=-1, keepdims=True)
        i2 = jnp.min(jnp.where(rest == v2, lane, big), axis=-1, keepdims=True)
        tot = v1 + v2
        comb_scr[...] = jnp.where(lane == i1, v1 / tot, jnp.where(lane == i2, v2 / tot, 0.0)) * g_w
        acc_scr[...] = jnp.zeros_like(acc_scr)

    def expert():
        hb = hb_scr[...]
        a = _dot(hb, wg_ref[0])
        u = _dot(hb, wu_ref[0])
        lane = lax.broadcasted_iota(jnp.int32, comb_scr.shape, 1)
        ce = jnp.sum(jnp.where(lane == e + N_GROUPS, comb_scr[...], 0.0), axis=-1, keepdims=True)
        act = (a * jax.nn.sigmoid(a) * u * ce).astype(BF16)
        acc_scr[...] += _dot(act, wd_ref[0])

    if fox_pages:
        step = lax.rem(pl.program_id(0) * N_EXPERTS + e, fox_steps)
        p = fox_pages
        _fox_sample_step(step, fox_steps, *fox_in[:N_FOX_FIXED_IN],
                         fox_in[N_FOX_FIXED_IN:N_FOX_FIXED_IN + p], fox_in[N_FOX_FIXED_IN + p:N_FOX_FIXED_IN + 2 * p],
                         fox_in[N_FOX_FIXED_IN + 2 * p:], y_ref, *fox_scr, alongside=expert)
    else:
        expert()

    @pl.when(e == N_EXPERTS - 1)
    def _finish():
        o_ref[...] = _rms(xp_scr[...] + gt2_ref[...] * acc_scr[...], gfin_ref[...])


def _post(x, ym, yf, gt1, sh2, sc2, gt2, g_norm2, g_final, wo_m, wo_f, w_route2, b_route, wg, wu, wd, tm, fox=None):
    t = x.shape[0]
    n_tiles = t // tm
    mod_rows = gt1.shape[0]
    mod_map = (lambda i, e, pt: (0, 0)) if mod_rows == 1 else (lambda i, e, pt: (i, 0))
    mod = pl.BlockSpec((1, D_MODEL) if mod_rows == 1 else (tm, D_MODEL), mod_map)
    tok = lambda w: pl.BlockSpec((tm, w), lambda i, e, pt: (i, 0))
    const = lambda shape: pl.BlockSpec(shape, lambda i, e, pt: (0,) * len(shape), pipeline_mode=pl.Buffered(1))
    expert = lambda shape: pl.BlockSpec((1,) + shape, lambda i, e, pt: (e, 0, 0))
    in_specs = [
        tok(D_MODEL), tok(W_M), tok(W_F), mod, mod, mod, mod,
        const((1, D_MODEL)), const((1, D_MODEL)),
        const(wo_m.shape), const(wo_f.shape), const(w_route2.shape), const((1, LANES)),
        expert((D_MODEL, D_FF_E)), expert((D_MODEL, D_FF_E)), expert((D_FF_E, D_MODEL)),
    ]
    operands = [x, ym, yf, gt1, sh2, sc2, gt2, g_norm2, g_final, wo_m, wo_f, w_route2, b_route, wg, wu, wd]
    out_specs = [tok(D_MODEL)]
    out_shape = [jax.ShapeDtypeStruct((t, D_MODEL), F32)]
    scratch = [pltpu.VMEM((tm, D_MODEL), F32), pltpu.VMEM((tm, D_MODEL), BF16),
               pltpu.VMEM((tm, LANES), F32), pltpu.VMEM((tm, D_MODEL), F32)]
    pages = steps = 0
    table = jnp.zeros((1,), jnp.int32)
    if fox is not None:
        page_table, fq, fk16, fv16, lf_new_t, g_out_f, cache_k, cache_v, cache_lf_t = fox
        nb, n_pages = page_table.shape
        pages, rem = divmod(nb * n_pages, n_tiles * N_EXPERTS)
        assert rem == 0 and pages > 0 and n_pages % pages == 0, (nb, n_pages, n_tiles)
        steps = n_pages // pages
        table = page_table.reshape(-1)
        seq_of = lambda i, e: (i * N_EXPERTS + e) // steps
        seq = lambda w: pl.BlockSpec((1, SAMPLE_ROWS, w), lambda i, e, pt: (seq_of(i, e), 0, 0))

        def page(shape, j):
            def index(i, e, pt):
                f = i * N_EXPERTS + e
                return (pt[(f // steps) * n_pages + (f % steps) * pages + j],) + (0,) * len(shape)
            return pl.BlockSpec((None,) + shape, index)

        in_specs += ([seq(W_F), seq(W_F), seq(W_F),
                      pl.BlockSpec((1, H_F, PAGE_SIZE), lambda i, e, pt: (seq_of(i, e), 0, 0)),
                      const((1, W_F))]
                     + [page((H_F, D_HF, PAGE_SIZE), j) for j in range(pages)] * 2
                     + [page((H_F, PAGE_SIZE), j) for j in range(pages)])
        operands += [fq, fk16, fv16, lf_new_t, g_out_f] + [cache_k] * pages + [cache_v] * pages + [cache_lf_t] * pages
        out_specs.append(seq(W_F))
        out_shape.append(jax.ShapeDtypeStruct((nb, SAMPLE_ROWS, W_F), BF16))
        scratch += [pltpu.VMEM((H_F * SAMPLE_ROWS, 1), F32), pltpu.VMEM((H_F * SAMPLE_ROWS, 1), F32),
                    pltpu.VMEM((H_F * SAMPLE_ROWS, D_HF), F32), pltpu.VMEM((H_F, 1), F32)]
    grid_spec = pltpu.PrefetchScalarGridSpec(
        num_scalar_prefetch=1, grid=(n_tiles, N_EXPERTS),
        in_specs=in_specs, out_specs=out_specs, scratch_shapes=scratch)
    outs = pl.pallas_call(
        functools.partial(_post_kernel, fox_pages=pages, fox_steps=steps),
        grid_spec=grid_spec,
        out_shape=out_shape,
        compiler_params=_params("arbitrary", "arbitrary"),
        name="post",
    )(table, *operands)
    return outs if fox is not None else outs[0]


def _pad_cols(w, n):
    return jnp.pad(w, ((0, 0), (0, n - w.shape[1])))


def _hi_lo_cols(w):
    hi, lo = _split_bf16(w)
    return jnp.concatenate([hi, lo], axis=1)


def _pad_seq(a):
    nb = a.shape[0] // DEC_SEQ
    return jnp.pad(a.reshape(nb, DEC_SEQ, a.shape[1]), ((0, 0), (0, SAMPLE_ROWS - DEC_SEQ), (0, 0)))


def _unpad_seq(a):
    return a[:, :DEC_SEQ].reshape(a.shape[0] * DEC_SEQ, a.shape[2])


def kernel(x_prompt, x_sample, c_prompt, c_sample, cache_k, cache_v, cache_logf, state_C, state_n, state_m,
           page_table, w_ada, b_ada, g_norm1, g_norm2, w_in, b_ig, b_fg, b_ff, g_out_m, g_out_f, w_out,
           w_group, b_group, w_router, b_router, w_gate, w_up, w_down, g_final):
    depth = w_ada.shape[0]
    assert depth == 1 and x_prompt.shape[0] == 1
    t = x_prompt.shape[1]
    nb, ds = x_sample.shape[:2]
    assert ds == DEC_SEQ
    ts = nb * ds
    xp = x_prompt.reshape(t, D_MODEL)
    xs = x_sample.reshape(ts, D_MODEL)
    l = 0

    w = w_in[l]
    o = np.cumsum([0, W_M, W_M, W_M, W_M, H_M, H_M, W_F, W_F, W_F, H_F])
    w_main = jnp.concatenate([w[:, o[0]:o[4]], w[:, o[6]:o[9]]], axis=1).astype(BF16)
    w_gates = _pad_cols(jnp.concatenate([w[:, o[4]:o[6]], w[:, o[9]:o[10]]], axis=1), GATE_COLS)
    w_gate2 = _hi_lo_cols(w_gates)
    gate_bias = _pad_cols(jnp.concatenate([b_ig[l], b_fg[l], b_ff[l]])[None, :], GATE_COLS)
    gm = g_out_m[l].reshape(1, W_M)
    gf = g_out_f[l].reshape(1, W_F)
    wo_m = w_out[l][:W_M].astype(BF16)
    wo_f = w_out[l][W_M:].astype(BF16)
    w_route = jnp.concatenate(
        [w_group[l], jnp.transpose(w_router[l], (1, 0, 2)).reshape(D_MODEL, N_EXPERTS)], axis=1)
    w_route2 = _hi_lo_cols(_pad_cols(w_route, LANES))
    b_route = _pad_cols(jnp.concatenate([b_group[l], b_router[l].reshape(-1)])[None, :], LANES)
    wg = w_gate[l].reshape(N_EXPERTS, D_MODEL, D_FF_E).astype(BF16)
    wu = w_up[l].reshape(N_EXPERTS, D_MODEL, D_FF_E).astype(BF16)
    wd = w_down[l].reshape(N_EXPERTS, D_FF_E, D_MODEL).astype(BF16)

    rows = -(-(1 + nb) // 8) * 8
    c_all = jnp.pad(jnp.concatenate([c_prompt, c_sample], axis=0), ((0, rows - 1 - nb), (0, 0)))
    mod = _adaln(c_all, w_ada[l], b_ada[l])
    p_mod = [mod[0:1, i * D_MODEL:(i + 1) * D_MODEL] for i in range(6)]
    s_mod = [jnp.repeat(mod[1:1 + nb, i * D_MODEL:(i + 1) * D_MODEL], ds, axis=0) for i in range(6)]
    g1 = g_norm1[l][None, :]
    g2 = g_norm2[l][None, :]
    gfin = g_final[None, :]

    smq, smk, smv, smo, sfq, sfk, sfv, sfk16, sfv16, sgates = _inproj(
        xs, s_mod[0], s_mod[1], g1, w_main, w_gate2, gate_bias, ts)
    m0 = jnp.broadcast_to(state_m[l][:, :, None], (nb, H_M, LANES))
    sym, c_s, n_s, m_s = _mlstm_sample(_pad_seq(smq), _pad_seq(smk), _pad_seq(smv), _pad_seq(smo), _pad_seq(sgates),
                                       gm, state_C[l], state_n[l], m0)
    lf_new = sgates[:, FOX_COL:FOX_COL + H_F]
    lf_new_t = jnp.pad(jnp.transpose(lf_new.reshape(nb, ds, H_F), (0, 2, 1)), ((0, 0), (0, 0), (0, PAGE_SIZE - ds)))
    fox = (page_table, _pad_seq(sfq), _pad_seq(sfk16), _pad_seq(sfv16), lf_new_t, gf,
           jnp.transpose(cache_k[l], (0, 2, 3, 1)), jnp.transpose(cache_v[l], (0, 2, 3, 1)),
           jnp.transpose(cache_logf[l], (0, 2, 1)))

    tm = min(TOKEN_TILE, t)
    tq = min(FOX_BLOCK, t)
    mq, mk, mv, mo, fq, fk, fv, fk16, fv16, gates = _inproj(xp, p_mod[0], p_mod[1], g1, w_main, w_gate2, gate_bias, tm)
    ym, fcol, c_p, nm_p = _mlstm_prompt(mq, mk, mv, mo, gates, gm)
    qat, ka, vat, stats = _fox_prep(fq, fk16, fv16, fcol, tq)
    yf = _fox_prompt(qat, ka, vat, stats, gf.reshape(W_F, 1), tq)
    y_prompt, syf = _post(xp, ym, yf, p_mod[2], p_mod[3], p_mod[4], p_mod[5], g2, gfin,
                          wo_m, wo_f, w_route2, b_route, wg, wu, wd, min(POST_TILE, t), fox=fox)
    y_sample = _post(xs, _unpad_seq(sym), _unpad_seq(syf), s_mod[2], s_mod[3], s_mod[4], s_mod[5], g2, gfin,
                     wo_m, wo_f, w_route2, b_route, wg, wu, wd, ts)

    return (
        y_prompt.reshape(1, t, D_MODEL), y_sample.reshape(nb, ds, D_MODEL),
        fk.reshape(1, 1, t, H_F, D_HF), fv.reshape(1, 1, t, H_F, D_HF),
        gates[:, FOX_COL:FOX_COL + H_F].reshape(1, 1, t, H_F),
        c_p[None, None], nm_p[None, None, :H_M], nm_p[None, None, H_M:, 0],
        sfk.reshape(1, nb, ds, H_F, D_HF), sfv.reshape(1, nb, ds, H_F, D_HF), lf_new.reshape(1, nb, ds, H_F),
        c_s[None], n_s[None], m_s[None, :, :, 0],
    )
```

```python
import functools
import math

import jax
import jax.numpy as jnp
import numpy as np
from jax import lax
from jax.experimental import pallas as pl
from jax.experimental.pallas import tpu as pltpu

F32 = jnp.float32
BF16 = jnp.bfloat16
HIGHEST = lax.Precision.HIGHEST

D_MODEL = 1024
H_M = 4
D_HM = 128
W_M = H_M * D_HM
H_F = 8
D_HF = 64
W_F = H_F * D_HF
PAGE_SIZE = 128
CHUNK = 128
DEC_SEQ = 4
N_GROUPS = 4
EXP_PER_GROUP = 4
N_EXPERTS = N_GROUPS * EXP_PER_GROUP
D_FF_E = D_MODEL // 4
FOX_SCALE = D_HF ** -0.5
LOG2E = math.log2(math.e)
EPS = 1e-6

LANES = 128
BF16_SUBLANES = 16
MXU_DIM = 256
GATE_COLS = LANES
FOX_COL = 2 * H_M
SAMPLE_ROWS = BF16_SUBLANES
AUGW = MXU_DIM
VROWS = D_HF + BF16_SUBLANES
TOKEN_TILE = 512
POST_TILE = 1024
FOX_BLOCK = 1024
STATS_ROWS = 16
SKIP_BELOW = 160.0
VMEM_LIMIT = 60000 * 1024


def _params(*sem):
    return pltpu.CompilerParams(dimension_semantics=sem, vmem_limit_bytes=VMEM_LIMIT)


def _dot(a, b):
    return jnp.dot(a, b, preferred_element_type=F32)


def _dot_nt(a, b):
    return lax.dot_general(a, b, (((1,), (1,)), ((), ())), preferred_element_type=F32)


def _dot_tn(a, b):
    return lax.dot_general(a, b, (((0,), (0,)), ((), ())), preferred_element_type=F32)


def _dot_hi(a, b):
    return jnp.dot(a, b, preferred_element_type=F32, precision=HIGHEST)


def _rms(x, g):
    return x * lax.rsqrt(jnp.mean(x * x, axis=-1, keepdims=True) + EPS) * g


def _log_sigmoid(x):
    return jnp.minimum(x, 0.0) - jnp.log1p(jnp.exp(-jnp.abs(x)))


def _split_bf16(x):
    hi = x.astype(BF16)
    lo = (x - hi.astype(F32)).astype(BF16)
    return hi, lo


def _split3(x):
    hi = x.astype(BF16).astype(F32)
    mid = (x - hi).astype(BF16).astype(F32)
    lo = (x - hi - mid).astype(BF16).astype(F32)
    return hi, mid, lo


def _adaln_kernel(c_ref, w_ref, b_ref, o_ref):
    c = c_ref[...]
    o_ref[...] = _dot_hi(c * jax.nn.sigmoid(c), w_ref[...]) + b_ref[...]


def _adaln(c_all, w_ada, b_ada):
    rows = c_all.shape[0]
    n = w_ada.shape[1]
    tn = 1536
    return pl.pallas_call(
        _adaln_kernel,
        grid=(n // tn,),
        in_specs=[
            pl.BlockSpec((rows, D_MODEL), lambda j: (0, 0)),
            pl.BlockSpec((D_MODEL, tn), lambda j: (0, j)),
            pl.BlockSpec((1, tn), lambda j: (0, j)),
        ],
        out_specs=pl.BlockSpec((rows, tn), lambda j: (0, j)),
        out_shape=jax.ShapeDtypeStruct((rows, n), F32),
        compiler_params=_params("arbitrary"),
        name="adaln",
    )(c_all, w_ada, b_ada.reshape(1, n))


def _inproj_kernel(x_ref, shift_ref, scale_ref, g_ref, w_ref, wg_ref, gb_ref,
                   mq_ref, mk_ref, mv_ref, mo_ref, fq_ref, fk_ref, fv_ref, fk16_ref, fv16_ref, gate_ref):
    h = _rms(x_ref[...], g_ref[...]) * (1.0 + scale_ref[...]) + shift_ref[...]
    hb, h_lo = _split_bf16(h)

    def proj(j):
        return _dot(hb, w_ref[:, j * W_M:(j + 1) * W_M])

    mq_ref[...] = proj(0).astype(BF16)
    mk_ref[...] = (proj(1) * (D_HM ** -0.5)).astype(BF16)
    mv_ref[...] = proj(2).astype(BF16)
    mo_ref[...] = jax.nn.sigmoid(proj(3))
    fq_ref[...] = (proj(4) * (FOX_SCALE * LOG2E)).astype(BF16)
    fk = proj(5)
    fk_ref[...] = fk
    fk16_ref[...] = fk.astype(BF16)
    fv = proj(6)
    fv_ref[...] = fv
    fv16_ref[...] = fv.astype(BF16)
    gg = _dot(hb, wg_ref[...])
    gpre = gg[:, :GATE_COLS] + gg[:, GATE_COLS:] + _dot(h_lo, wg_ref[:, :GATE_COLS]) + gb_ref[...]
    lane = lax.broadcasted_iota(jnp.int32, gpre.shape, 1)
    gate_ref[...] = jnp.where(lane < H_M, gpre, jnp.where(lane < FOX_COL + H_F, _log_sigmoid(gpre), 0.0))


def _inproj(x, shift, scale, g_norm, w_main, w_gate2, gate_bias, tm):
    t = x.shape[0]
    mod_rows = shift.shape[0]
    mod_map = (lambda i: (0, 0)) if mod_rows == 1 else (lambda i: (i, 0))
    mod_block = (1, D_MODEL) if mod_rows == 1 else (tm, D_MODEL)
    tok = lambda w: pl.BlockSpec((tm, w), lambda i: (i, 0))
    const = lambda shape: pl.BlockSpec(shape, lambda i: (0, 0))
    out_shapes = (
        jax.ShapeDtypeStruct((t, W_M), BF16), jax.ShapeDtypeStruct((t, W_M), BF16),
        jax.ShapeDtypeStruct((t, W_M), BF16), jax.ShapeDtypeStruct((t, W_M), F32),
        jax.ShapeDtypeStruct((t, W_F), BF16), jax.ShapeDtypeStruct((t, W_F), F32),
        jax.ShapeDtypeStruct((t, W_F), F32), jax.ShapeDtypeStruct((t, W_F), BF16),
        jax.ShapeDtypeStruct((t, W_F), BF16), jax.ShapeDtypeStruct((t, GATE_COLS), F32),
    )
    return pl.pallas_call(
        _inproj_kernel,
        grid=(t // tm,),
        in_specs=[
            tok(D_MODEL),
            pl.BlockSpec(mod_block, mod_map), pl.BlockSpec(mod_block, mod_map),
            const((1, D_MODEL)), const(w_main.shape), const(w_gate2.shape), const((1, GATE_COLS)),
        ],
        out_specs=[tok(W_M)] * 4 + [tok(W_F)] * 5 + [tok(GATE_COLS)],
        out_shape=out_shapes,
        compiler_params=_params("arbitrary"),
        name="inproj",
    )(x, shift, scale, g_norm, w_main, w_gate2, gate_bias)


def _mlstm_chunk(rows, q_ref, k_ref, v_ref, o_ref, g_ref, gout_ref, c_ref, nm_ref, fcarry_ref):
    L = CHUNK
    row = lax.broadcasted_iota(jnp.int32, (L, L), 0)
    col = lax.broadcasted_iota(jnp.int32, (L, L), 1)
    causal = row >= col
    gates = g_ref[rows, :]
    cum = _dot_hi(causal.astype(F32), gates)
    gates_t = gates.T
    cum_t = cum.T
    outs = []
    for h in range(H_M):
        sl = slice(h * D_HM, (h + 1) * D_HM)
        a_row = gates_t[h:h + 1, :] - cum_t[H_M + h:H_M + h + 1, :]
        b_col = cum[:, H_M + h:H_M + h + 1]
        a_col = gates[:, h:h + 1] - b_col
        m_prev = nm_ref[H_M + h:H_M + h + 1, 0:1]
        a_mat = jnp.where(causal, a_row, -jnp.inf)
        r = jnp.maximum(jnp.max(a_mat, axis=-1, keepdims=True), m_prev)
        w_intra = jnp.exp(a_mat - r)
        w_inter = jnp.exp(m_prev - r)
        q = q_ref[rows, sl]
        k = k_ref[rows, sl]
        v = v_ref[rows, sl]
        s = _dot_nt(q, k) * w_intra
        c_old = c_ref[h]
        n_old = nm_ref[h:h + 1, :]
        num = _dot(s.astype(BF16), v) + w_inter * _dot_nt(q, c_old.astype(BF16))
        den = (jnp.sum(s, axis=-1, keepdims=True)
               + w_inter * jnp.sum(q.astype(F32) * n_old, axis=-1, keepdims=True))
        m_t = b_col + r
        hid = num / jnp.maximum(jnp.abs(den), jnp.exp(-m_t))
        m_new = m_t[L - 1:L, :]
        b_last = b_col[L - 1:L, :]
        w_state = jnp.exp(a_col + (b_last - m_new))
        decay = jnp.exp(b_last + m_prev - m_new)
        vw = (v.astype(F32) * w_state).astype(BF16)
        c_ref[h] = decay * c_old + _dot_tn(vw, k)
        nm_ref[h:h + 1, :] = decay * n_old + jnp.sum(k.astype(F32) * w_state, axis=0, keepdims=True)
        nm_ref[H_M + h:H_M + h + 1, :] = jnp.broadcast_to(m_new, (1, LANES))
        outs.append(_rms(o_ref[rows, sl] * hid, gout_ref[:, sl]).astype(BF16))
    f_glob = cum + fcarry_ref[...]
    fcarry_ref[...] = f_glob[L - 1:L, :]
    return jnp.concatenate(outs, axis=1), f_glob


def _prompt_mixer_kernel(x_ref, shift_ref, scale_ref, g_ref, w_ref, wg_ref, gb_ref, gout_ref,
                         fq_ref, fk_ref, fv_ref, fk16_ref, fv16_ref, gate_ref, ym_ref, fcol_ref, c_ref, nm_ref,
                         q_a, k_a, v_a, o_a, g_a, q_b, k_b, v_b, o_b, g_b, fcarry_ref, *, tm):
    i = pl.program_id(0)

    @pl.when(i == 0)
    def _init():
        for ref in (c_ref, nm_ref, fcarry_ref, q_b, k_b, v_b, o_b, g_b):
            ref[...] = jnp.zeros_like(ref)

    def step(write, read):
        wq, wk, wv, wo, wgt = write
        h = _rms(x_ref[...], g_ref[...]) * (1.0 + scale_ref[...]) + shift_ref[...]
        hb, h_lo = _split_bf16(h)

        def proj(j):
            return _dot(hb, w_ref[:, j * W_M:(j + 1) * W_M])

        def scan(c):
            rows = slice(c * CHUNK, (c + 1) * CHUNK)
            ym, f_glob = _mlstm_chunk(rows, *read, gout_ref, c_ref, nm_ref, fcarry_ref)
            ym_ref[rows, :] = ym
            fcol_ref[rows, :] = f_glob

        def stage0():
            wq[...] = proj(0).astype(BF16)
            wk[...] = (proj(1) * (D_HM ** -0.5)).astype(BF16)

        def stage1():
            wv[...] = proj(2).astype(BF16)
            wo[...] = jax.nn.sigmoid(proj(3))

        def stage2():
            fq_ref[...] = (proj(4) * (FOX_SCALE * LOG2E)).astype(BF16)
            fk = proj(5)
            fk_ref[...] = fk
            fk16_ref[...] = fk.astype(BF16)

        def stage3():
            fv = proj(6)
            fv_ref[...] = fv
            fv16_ref[...] = fv.astype(BF16)
            gg = _dot(hb, wg_ref[...])
            gpre = gg[:, :GATE_COLS] + gg[:, GATE_COLS:] + _dot(h_lo, wg_ref[:, :GATE_COLS]) + gb_ref[...]
            lane = lax.broadcasted_iota(jnp.int32, gpre.shape, 1)
            gts = jnp.where(lane < H_M, gpre, jnp.where(lane < FOX_COL + H_F, _log_sigmoid(gpre), 0.0))
            gate_ref[...] = gts
            wgt[...] = gts

        stages = (stage0, stage1, stage2, stage3)
        n_chunks = tm // CHUNK
        for c in range(max(n_chunks, len(stages))):
            if c < len(stages):
                stages[c]()
            if c < n_chunks:
                scan(c)

    buf_a = (q_a, k_a, v_a, o_a, g_a)
    buf_b = (q_b, k_b, v_b, o_b, g_b)

    @pl.when(lax.rem(i, 2) == 0)
    def _even():
        step(buf_a, buf_b)

    @pl.when(lax.rem(i, 2) == 1)
    def _odd():
        step(buf_b, buf_a)


def _prompt_mixer(x, shift, scale, g_norm, w_main, w_gate2, gate_bias, g_out_m, tm):
    t = x.shape[0]
    n = t // tm
    cur = lambda w: pl.BlockSpec((tm, w), lambda i: (jnp.minimum(i, n - 1), 0))
    prev = lambda w: pl.BlockSpec((tm, w), lambda i: (jnp.maximum(i - 1, 0), 0))
    const = lambda shape: pl.BlockSpec(shape, lambda i: (0,) * len(shape))
    tile = lambda w, dt: pltpu.VMEM((tm, w), dt)
    buf = [tile(W_M, BF16), tile(W_M, BF16), tile(W_M, BF16), tile(W_M, F32), tile(GATE_COLS, F32)]
    return pl.pallas_call(
        functools.partial(_prompt_mixer_kernel, tm=tm),
        grid=(n + 1,),
        in_specs=[cur(D_MODEL), const((1, D_MODEL)), const((1, D_MODEL)), const((1, D_MODEL)),
                  const(w_main.shape), const(w_gate2.shape), const((1, GATE_COLS)), const((1, W_M))],
        out_specs=[cur(W_F)] * 5 + [cur(GATE_COLS), prev(W_M), prev(GATE_COLS),
                                    const((H_M, D_HM, D_HM)), const((2 * H_M, D_HM))],
        out_shape=(
            jax.ShapeDtypeStruct((t, W_F), BF16), jax.ShapeDtypeStruct((t, W_F), F32),
            jax.ShapeDtypeStruct((t, W_F), F32), jax.ShapeDtypeStruct((t, W_F), BF16),
            jax.ShapeDtypeStruct((t, W_F), BF16), jax.ShapeDtypeStruct((t, GATE_COLS), F32),
            jax.ShapeDtypeStruct((t, W_M), BF16), jax.ShapeDtypeStruct((t, GATE_COLS), F32),
            jax.ShapeDtypeStruct((H_M, D_HM, D_HM), F32), jax.ShapeDtypeStruct((2 * H_M, D_HM), F32),
        ),
        scratch_shapes=buf + buf + [pltpu.VMEM((1, GATE_COLS), F32)],
        compiler_params=_params("arbitrary"),
        name="prompt_mixer",
    )(x, shift, scale, g_norm, w_main, w_gate2, gate_bias, g_out_m)


def _fox_prep_kernel(q_ref, k_ref, v_ref, fcol_ref, qat_ref, ka_ref, vat_ref, stats_ref):
    tm = q_ref.shape[0]
    lane = lax.broadcasted_iota(jnp.int32, (tm, LANES), 1)
    lo = lane < D_HF
    f_all = fcol_ref[...] * LOG2E
    ones_rows = jnp.where(lax.broadcasted_iota(jnp.int32, (VROWS - D_HF, tm), 0) == 0, 1.0, 0.0).astype(BF16)
    half_of_lane = lax.broadcasted_iota(jnp.int32, (LANES, LANES), 0) // D_HF
    half_sum = (half_of_lane == lax.broadcasted_iota(jnp.int32, (LANES, LANES), 1)).astype(BF16)
    q_norms, k_norms = [], []
    for p in range(H_F // 2):
        sl = slice(p * LANES, (p + 1) * LANES)
        q2 = q_ref[:, sl].astype(F32)
        k2 = k_ref[:, sl].astype(F32)
        q_norms.append(jnp.max(_dot((q2 * q2).astype(BF16), half_sum), axis=0, keepdims=True))
        k_norms.append(jnp.max(_dot((k2 * k2).astype(BF16), half_sum), axis=0, keepdims=True))
        k_aug = jnp.zeros((tm, LANES), F32)
        for half in range(2):
            h = 2 * p + half
            hi, mid, lw = _split3(f_all[:, FOX_COL + h:FOX_COL + h + 1])
            b = 8 * half
            q_aug = jnp.where(lane == b, hi, jnp.where(lane == b + 1, mid, jnp.where(lane == b + 2, lw, jnp.where(
                (lane >= b + 3) & (lane < b + 6), 1.0, 0.0))))
            k_aug = k_aug + jnp.where((lane >= b) & (lane < b + 3), 1.0, jnp.where(
                lane == b + 3, -hi, jnp.where(lane == b + 4, -mid, jnp.where(lane == b + 5, -lw, 0.0))))
            q_own = jnp.where(lo, q2, 0.0) if half == 0 else jnp.where(lo, 0.0, q2)
            qat_ref[h * AUGW:h * AUGW + LANES, :] = q_own.T.astype(BF16)
            qat_ref[h * AUGW + LANES:(h + 1) * AUGW, :] = q_aug.T.astype(BF16)
        ka_ref[:, p * AUGW:p * AUGW + LANES] = k_ref[:, sl]
        ka_ref[:, p * AUGW + LANES:(p + 1) * AUGW] = k_aug.astype(BF16)
        vt = v_ref[:, sl].astype(F32).T.astype(BF16)
        for half in range(2):
            h = 2 * p + half
            vat_ref[h * VROWS:h * VROWS + D_HF, :] = vt[half * D_HF:(half + 1) * D_HF, :]
            vat_ref[h * VROWS + D_HF:(h + 1) * VROWS, :] = ones_rows
    pad = jnp.zeros((STATS_ROWS - H_F - 2, LANES), F32)
    stats_ref[0] = jnp.concatenate(q_norms + k_norms + [f_all[0:1, :], f_all[tm - 1:tm, :], pad], axis=0)


def _fox_prep(fq, fk16, fv16, fcol, tm):
    t = fq.shape[0]
    tok = lambda w: pl.BlockSpec((tm, w), lambda i: (i, 0))
    tr = lambda r: pl.BlockSpec((r, tm), lambda i: (0, i))
    return pl.pallas_call(
        _fox_prep_kernel,
        grid=(t // tm,),
        in_specs=[tok(W_F), tok(W_F), tok(W_F), tok(GATE_COLS)],
        out_specs=[tr(H_F * AUGW), tok(H_F // 2 * AUGW), tr(H_F * VROWS),
                   pl.BlockSpec((1, STATS_ROWS, LANES), lambda i: (i, 0, 0))],
        out_shape=(jax.ShapeDtypeStruct((H_F * AUGW, t), BF16), jax.ShapeDtypeStruct((t, H_F // 2 * AUGW), BF16),
                   jax.ShapeDtypeStruct((H_F * VROWS, t), BF16), jax.ShapeDtypeStruct((t // tm, STATS_ROWS, LANES), F32)),
        compiler_params=_params("arbitrary"),
        name="fox_prep",
    )(fq, fk16, fv16, fcol)


def _fox_prompt_kernel(qtab_ref, ktab_ref, live_ref, kfetch_ref, qat_ref, ka_ref, vat_ref, gcol_ref,
                       y_ref, m_scr, acc_scr, *, tq):
    i = pl.program_id(0)
    qb = qtab_ref[i]
    kb = ktab_ref[i]

    @pl.when(kb == 0)
    def _init():
        m_scr[...] = jnp.full_like(m_scr, -jnp.inf)
        acc_scr[...] = jnp.zeros_like(acc_scr)

    def head_update(h, diagonal):
        p = h // 2
        s = _dot(ka_ref[:, p * AUGW:(p + 1) * AUGW], qat_ref[h * AUGW:(h + 1) * AUGW, :])
        if diagonal:
            keep = (lax.broadcasted_iota(jnp.int32, (tq, tq), 0) <= lax.broadcasted_iota(jnp.int32, (tq, tq), 1))
            s = jnp.where(keep, s, -jnp.inf)
        m_prev = m_scr[h]
        m_new = jnp.maximum(m_prev, jnp.max(s, axis=0, keepdims=True))
        m_scr[h] = m_new
        prob_t = jnp.exp2(s - m_new).astype(BF16)
        acc_scr[h] = jnp.exp2(m_prev - m_new) * acc_scr[h] + _dot(vat_ref[h * VROWS:(h + 1) * VROWS, :], prob_t)

    @pl.when(kb < qb)
    def _off_diagonal():
        for h in range(H_F):
            @pl.when(live_ref[i * H_F + h] != 0)
            def _live_head():
                head_update(h, False)

    @pl.when(kb == qb)
    def _diagonal():
        for h in range(H_F):
            head_update(h, True)
        for p in range(H_F // 2):
            outs = []
            for h in (2 * p, 2 * p + 1):
                acc = acc_scr[h]
                out = acc[:D_HF, :] / acc[D_HF:D_HF + 1, :]
                ms = jnp.mean(out * out, axis=0, keepdims=True)
                outs.append(out * lax.rsqrt(ms + EPS) * gcol_ref[h * D_HF:(h + 1) * D_HF, :])
            y_ref[:, p * LANES:(p + 1) * LANES] = jnp.concatenate(outs, axis=0).T.astype(BF16)


def _fox_live(stats, qtab, ktab):
    slack = 1.0 + 2.0 ** -6
    qn = jnp.sqrt(stats[:, 0:H_F // 2, 0:2].reshape(-1, H_F) * slack)
    kn = jnp.sqrt(stats[:, H_F // 2:H_F, 0:2].reshape(-1, H_F) * slack)
    f_first = stats[:, H_F, FOX_COL:FOX_COL + H_F]
    f_last = stats[:, H_F + 1, FOX_COL:FOX_COL + H_F]
    bound = qn[qtab] * (kn[ktab] + kn[qtab]) + f_first[qtab] - f_last[ktab]
    dead = (bound < -SKIP_BELOW) & (ktab < qtab - 1)[:, None]
    return jnp.logical_not(dead).astype(jnp.int32).reshape(-1)


def _fox_prompt(qat, ka, vat, stats, g_col, tq):
    t = ka.shape[0]
    nb = t // tq
    pairs = [(q, k) for q in range(nb) for k in range(q + 1)]
    qtab = np.array([p[0] for p in pairs], np.int32)
    ktab = np.array([p[1] for p in pairs], np.int32)
    live = _fox_live(stats, qtab, ktab)
    fetch = jnp.any(live.reshape(len(pairs), H_F) != 0, axis=1) | (ktab == qtab)
    kfetch = jnp.asarray(ktab)[lax.cummax(jnp.where(fetch, jnp.arange(len(pairs)), 0))]
    grid_spec = pltpu.PrefetchScalarGridSpec(
        num_scalar_prefetch=4,
        grid=(len(pairs),),
        in_specs=[
            pl.BlockSpec((H_F * AUGW, tq), lambda i, qt, kt, lv, kf: (0, qt[i])),
            pl.BlockSpec((tq, H_F // 2 * AUGW), lambda i, qt, kt, lv, kf: (kf[i], 0)),
            pl.BlockSpec((H_F * VROWS, tq), lambda i, qt, kt, lv, kf: (0, kf[i])),
            pl.BlockSpec((W_F, 1), lambda i, qt, kt, lv, kf: (0, 0)),
        ],
        out_specs=pl.BlockSpec((tq, W_F), lambda i, qt, kt, lv, kf: (qt[i], 0)),
        scratch_shapes=[pltpu.VMEM((H_F, 1, tq), F32), pltpu.VMEM((H_F, VROWS, tq), F32)],
    )
    return pl.pallas_call(
        functools.partial(_fox_prompt_kernel, tq=tq),
        grid_spec=grid_spec,
        out_shape=jax.ShapeDtypeStruct((t, W_F), BF16),
        compiler_params=_params("arbitrary"),
        name="fox_prompt",
    )(jnp.asarray(qtab), jnp.asarray(ktab), live, kfetch, qat, ka, vat, g_col)


def _mlstm_sample_kernel(q_ref, k_ref, v_ref, o_ref, g_ref, gout_ref, c0_ref, n0_ref, m0_ref,
                         ym_ref, c_ref, n_ref, m_ref):
    R = SAMPLE_ROWS
    gates = g_ref[0]
    rows = [gates[0:1, :]]
    for t in range(1, DEC_SEQ):
        rows.append(rows[-1] + gates[t:t + 1, :])
    rows += [rows[-1]] * (R - DEC_SEQ)
    cum = jnp.concatenate(rows, axis=0)
    ridx = lax.broadcasted_iota(jnp.int32, (R, 1), 0)
    for h in range(H_M):
        sl = slice(h * D_HM, (h + 1) * D_HM)
        b_col = cum[:, H_M + h:H_M + h + 1]
        a_col = gates[:, h:h + 1] - b_col
        m_prev = m0_ref[0, h:h + 1, 0:1]
        cm = jnp.full((R, 1), -jnp.inf, F32)
        for s in range(DEC_SEQ):
            cm = jnp.maximum(cm, jnp.where(ridx >= s, a_col[s:s + 1, :], -jnp.inf))
        r = jnp.maximum(cm, m_prev)
        w_inter = jnp.exp(m_prev - r)
        q = q_ref[0, :, sl]
        k = k_ref[0, :, sl]
        v = v_ref[0, :, sl]
        q32, k32, v32 = q.astype(F32), k.astype(F32), v.astype(F32)
        c_old = c0_ref[0, h]
        n_old = n0_ref[0, h:h + 1, :]
        num = w_inter * _dot_nt(q, c_old.astype(BF16))
        den = w_inter * jnp.sum(q32 * n_old, axis=-1, keepdims=True)
        for s in range(DEC_SEQ):
            w_s = jnp.where(ridx >= s, jnp.exp(a_col[s:s + 1, :] - r), 0.0)
            s_col = jnp.sum(q32 * k32[s:s + 1, :], axis=-1, keepdims=True) * w_s
            num = num + s_col * v32[s:s + 1, :]
            den = den + s_col
        m_t = b_col + r
        hid = num / jnp.maximum(jnp.abs(den), jnp.exp(-m_t))
        m_new = m_t[DEC_SEQ - 1:DEC_SEQ, :]
        b_last = b_col[DEC_SEQ - 1:DEC_SEQ, :]
        w_state = jnp.where(ridx < DEC_SEQ, jnp.exp(a_col + (b_last - m_new)), 0.0)
        decay = jnp.exp(b_last + m_prev - m_new)
        pad = jnp.zeros((D_HM - R, D_HM), BF16)
        vw = jnp.concatenate([(v32 * w_state).astype(BF16), pad], axis=0)
        kp = jnp.concatenate([k, pad], axis=0)
        c_ref[0, h] = decay * c_old + _dot_tn(vw, kp)
        n_ref[0, h:h + 1, :] = decay * n_old + jnp.sum(k32 * w_state, axis=0, keepdims=True)
        m_ref[0, h:h + 1, :] = jnp.broadcast_to(m_new, (1, LANES))
        ym_ref[0, :, sl] = _rms(o_ref[0, :, sl] * hid, gout_ref[:, sl]).astype(BF16)


def _mlstm_sample(mq, mk, mv, mo, gates, g_out_m, c0, n0, m0):
    nb = mq.shape[0]
    seq = lambda w: pl.BlockSpec((1, SAMPLE_ROWS, w), lambda b: (b, 0, 0))
    st_c = pl.BlockSpec((1, H_M, D_HM, D_HM), lambda b: (b, 0, 0, 0))
    st_v = pl.BlockSpec((1, H_M, D_HM), lambda b: (b, 0, 0))
    return pl.pallas_call(
        _mlstm_sample_kernel,
        grid=(nb,),
        in_specs=[seq(W_M), seq(W_M), seq(W_M), seq(W_M), seq(GATE_COLS),
                  pl.BlockSpec((1, W_M), lambda b: (0, 0)), st_c, st_v, st_v],
        out_specs=[seq(W_M), st_c, st_v, st_v],
        out_shape=(
            jax.ShapeDtypeStruct((nb, SAMPLE_ROWS, W_M), BF16),
            jax.ShapeDtypeStruct((nb, H_M, D_HM, D_HM), F32),
            jax.ShapeDtypeStruct((nb, H_M, D_HM), F32), jax.ShapeDtypeStruct((nb, H_M, D_HM), F32),
        ),
        compiler_params=_params("arbitrary"),
        name="mlstm_sample",
    )(mq, mk, mv, mo, gates, g_out_m, c0, n0, m0)


def _fox_sample_step(step, n_steps, q_ref, knew_ref, vnew_ref, lfnew_ref, gout_ref, k_refs, v_refs, lf_refs,
                     y_ref, m_scr, l_scr, acc_scr, carry_scr, alongside):
    pages = len(k_refs)
    r = SAMPLE_ROWS

    @pl.when(step == 0)
    def _init():
        m_scr[...] = jnp.full_like(m_scr, -jnp.inf)
        l_scr[...] = jnp.zeros_like(l_scr)
        acc_scr[...] = jnp.zeros_like(acc_scr)
        carry_scr[...] = jnp.zeros_like(carry_scr)

    def absorb(scores, pv):
        s = jnp.concatenate(scores, axis=0)
        m_prev = m_scr[...]
        m_new = jnp.maximum(m_prev, jnp.max(s, axis=-1, keepdims=True))
        alpha = jnp.exp2(m_prev - m_new)
        prob = jnp.exp2(s - m_new)
        l_scr[...] = alpha * l_scr[...] + jnp.sum(prob, axis=-1, keepdims=True)
        pb = prob.astype(BF16)
        acc_scr[...] = alpha * acc_scr[...] + jnp.concatenate(
            [pv(h, pb[h * r:(h + 1) * r, :]) for h in range(H_F)], axis=0)
        m_scr[...] = m_new

    upper = (lax.broadcasted_iota(jnp.int32, (PAGE_SIZE, PAGE_SIZE), 0)
             <= lax.broadcasted_iota(jnp.int32, (PAGE_SIZE, PAGE_SIZE), 1)).astype(F32)
    cum = _dot_hi(jnp.concatenate([lf_refs[j][...] for j in range(pages)], axis=0), upper)
    carry = carry_scr[...]
    f_past = []
    for j in range(pages):
        f_past.append(carry + cum[j * H_F:(j + 1) * H_F, :])
        carry = f_past[-1][:, PAGE_SIZE - 1:PAGE_SIZE]
    carry_scr[...] = carry
    heads = [slice(h * D_HF, (h + 1) * D_HF) for h in range(H_F)]
    scores, v_t = [], []
    for h in range(H_F):
        k_t = jnp.concatenate([k_refs[j][h] for j in range(pages)], axis=1).astype(BF16)
        v_t.append(jnp.concatenate([v_refs[j][h] for j in range(pages)], axis=1).astype(BF16))
        bias = jnp.concatenate([f_past[j][h:h + 1, :] for j in range(pages)], axis=1) * LOG2E
        scores.append(_dot(q_ref[0, :, heads[h]], k_t) - bias)
    alongside()
    absorb(scores, lambda h, prob: _dot_nt(prob, v_t[h]))

    @pl.when(step == n_steps - 1)
    def _finish():
        cn = _dot_hi(lfnew_ref[0], upper)[:, :r]
        causal = lax.broadcasted_iota(jnp.int32, (r, r), 1) <= lax.broadcasted_iota(jnp.int32, (r, r), 0)
        scores = []
        for h in range(H_F):
            bias = (cn[h:h + 1, :] + carry[h:h + 1, :]) * LOG2E
            scores.append(jnp.where(causal, _dot_nt(q_ref[0, :, heads[h]], knew_ref[0, :, heads[h]]) - bias, -jnp.inf))
        absorb(scores, lambda h, prob: _dot(prob, vnew_ref[0, :, heads[h]]))
        out = acc_scr[...] / l_scr[...]
        y_ref[0] = jnp.concatenate(
            [_rms(out[h * r:(h + 1) * r, :], gout_ref[:, heads[h]]) for h in range(H_F)], axis=1).astype(BF16)


N_POST_IN = 16
N_FOX_FIXED_IN = 5


def _post_kernel(pt_ref, *refs, fox_pages, fox_steps):
    (x_ref, ym_ref, yf_ref, gt1_ref, sh2_ref, sc2_ref, gt2_ref, g2_ref, gfin_ref,
     wom_ref, wof_ref, wr_ref, br_ref, wg_ref, wu_ref, wd_ref) = refs[:N_POST_IN]
    n_fox_in = N_FOX_FIXED_IN + 3 * fox_pages if fox_pages else 0
    fox_in = refs[N_POST_IN:N_POST_IN + n_fox_in]
    outs = refs[N_POST_IN + n_fox_in:]
    if fox_pages:
        o_ref, y_ref, xp_scr, hb_scr, comb_scr, acc_scr = outs[:6]
        fox_scr = outs[6:]
    else:
        o_ref, xp_scr, hb_scr, comb_scr, acc_scr = outs
    e = pl.program_id(1)

    @pl.when(e == 0)
    def _prepare():
        mix = _dot(ym_ref[...], wom_ref[...]) + _dot(yf_ref[...], wof_ref[...])
        xp = x_ref[...] + gt1_ref[...] * mix
        xp_scr[...] = xp
        h = _rms(xp, g2_ref[...]) * (1.0 + sc2_ref[...]) + sh2_ref[...]
        hb, h_lo = _split_bf16(h)
        hb_scr[...] = hb
        rr = _dot(hb, wr_ref[...])
        logits = rr[:, :LANES] + rr[:, LANES:] + _dot(h_lo, wr_ref[:, :LANES]) + br_ref[...]
        lane = lax.broadcasted_iota(jnp.int32, logits.shape, 1)
        big = jnp.int32(4 * LANES)
        gl = jnp.where(lane < N_GROUPS, logits, -jnp.inf)
        gmax = jnp.max(gl, axis=-1, keepdims=True)
        gidx = jnp.min(jnp.where(gl == gmax, lane, big), axis=-1, keepdims=True)
        g_w = 1.0 / jnp.sum(jnp.exp(gl - gmax), axis=-1, keepdims=True)
        first = N_GROUPS + EXP_PER_GROUP * gidx
        el = jnp.where((lane >= first) & (lane < first + EXP_PER_GROUP), logits, -jnp.inf)
        emax = jnp.max(el, axis=-1, keepdims=True)
        pe = jnp.exp(el - emax)
        prob = pe / jnp.sum(pe, axis=-1, keepdims=True)
        v1 = jnp.max(prob, axis=-1, keepdims=True)
        i1 = jnp.min(jnp.where(prob == v1, lane, big), axis=-1, keepdims=True)
        rest = jnp.where((lane == i1) | (el == -jnp.inf), -1.0, prob)
        v2 = jnp.max(rest, axis=-1, keepdims=True)
        i2 = jnp.min(jnp.where(rest == v2, lane, big), axis=-1, keepdims=True)
        tot = v1 + v2
        comb_scr[...] = jnp.where(lane == i1, v1 / tot, jnp.where(lane == i2, v2 / tot, 0.0)) * g_w
        acc_scr[...] = jnp.zeros_like(acc_scr)

    def expert():
        hb = hb_scr[...]
        a = _dot(hb, wg_ref[0])
        u = _dot(hb, wu_ref[0])
        lane = lax.broadcasted_iota(jnp.int32, comb_scr.shape, 1)
        ce = jnp.sum(jnp.where(lane == e + N_GROUPS, comb_scr[...], 0.0), axis=-1, keepdims=True)
        act = (a * jax.nn.sigmoid(a) * u * ce).astype(BF16)
        acc_scr[...] += _dot(act, wd_ref[0])

    if fox_pages:
        step = lax.rem(pl.program_id(0) * N_EXPERTS + e, fox_steps)
        p = fox_pages
        _fox_sample_step(step, fox_steps, *fox_in[:N_FOX_FIXED_IN],
                         fox_in[N_FOX_FIXED_IN:N_FOX_FIXED_IN + p], fox_in[N_FOX_FIXED_IN + p:N_FOX_FIXED_IN + 2 * p],
                         fox_in[N_FOX_FIXED_IN + 2 * p:], y_ref, *fox_scr, alongside=expert)
    else:
        expert()

    @pl.when(e == N_EXPERTS - 1)
    def _finish():
        o_ref[...] = _rms(xp_scr[...] + gt2_ref[...] * acc_scr[...], gfin_ref[...])


def _post(x, ym, yf, gt1, sh2, sc2, gt2, g_norm2, g_final, wo_m, wo_f, w_route2, b_route, wg, wu, wd, tm, fox=None):
    t = x.shape[0]
    n_tiles = t // tm
    mod_rows = gt1.shape[0]
    mod_map = (lambda i, e, pt: (0, 0)) if mod_rows == 1 else (lambda i, e, pt: (i, 0))
    mod = pl.BlockSpec((1, D_MODEL) if mod_rows == 1 else (tm, D_MODEL), mod_map)
    tok = lambda w: pl.BlockSpec((tm, w), lambda i, e, pt: (i, 0))
    const = lambda shape: pl.BlockSpec(shape, lambda i, e, pt: (0,) * len(shape), pipeline_mode=pl.Buffered(1))
    expert = lambda shape: pl.BlockSpec((1,) + shape, lambda i, e, pt: (e, 0, 0))
    in_specs = [
        tok(D_MODEL), tok(W_M), tok(W_F), mod, mod, mod, mod,
        const((1, D_MODEL)), const((1, D_MODEL)),
        const(wo_m.shape), const(wo_f.shape), const(w_route2.shape), const((1, LANES)),
        expert((D_MODEL, D_FF_E)), expert((D_MODEL, D_FF_E)), expert((D_FF_E, D_MODEL)),
    ]
    operands = [x, ym, yf, gt1, sh2, sc2, gt2, g_norm2, g_final, wo_m, wo_f, w_route2, b_route, wg, wu, wd]
    out_specs = [tok(D_MODEL)]
    out_shape = [jax.ShapeDtypeStruct((t, D_MODEL), F32)]
    scratch = [pltpu.VMEM((tm, D_MODEL), F32), pltpu.VMEM((tm, D_MODEL), BF16),
               pltpu.VMEM((tm, LANES), F32), pltpu.VMEM((tm, D_MODEL), F32)]
    pages = steps = 0
    table = jnp.zeros((1,), jnp.int32)
    if fox is not None:
        page_table, fq, fk16, fv16, lf_new_t, g_out_f, cache_k, cache_v, cache_lf_t = fox
        nb, n_pages = page_table.shape
        pages, rem = divmod(nb * n_pages, n_tiles * N_EXPERTS)
        assert rem == 0 and pages > 0 and n_pages % pages == 0, (nb, n_pages, n_tiles)
        steps = n_pages // pages
        table = page_table.reshape(-1)
        seq_of = lambda i, e: (i * N_EXPERTS + e) // steps
        seq = lambda w: pl.BlockSpec((1, SAMPLE_ROWS, w), lambda i, e, pt: (seq_of(i, e), 0, 0))

        def page(shape, j):
            def index(i, e, pt):
                f = i * N_EXPERTS + e
                return (pt[(f // steps) * n_pages + (f % steps) * pages + j],) + (0,) * len(shape)
            return pl.BlockSpec((None,) + shape, index)

        in_specs += ([seq(W_F), seq(W_F), seq(W_F),
                      pl.BlockSpec((1, H_F, PAGE_SIZE), lambda i, e, pt: (seq_of(i, e), 0, 0)),
                      const((1, W_F))]
                     + [page((H_F, D_HF, PAGE_SIZE), j) for j in range(pages)] * 2
                     + [page((H_F, PAGE_SIZE), j) for j in range(pages)])
        operands += [fq, fk16, fv16, lf_new_t, g_out_f] + [cache_k] * pages + [cache_v] * pages + [cache_lf_t] * pages
        out_specs.append(seq(W_F))
        out_shape.append(jax.ShapeDtypeStruct((nb, SAMPLE_ROWS, W_F), BF16))
        scratch += [pltpu.VMEM((H_F * SAMPLE_ROWS, 1), F32), pltpu.VMEM((H_F * SAMPLE_ROWS, 1), F32),
                    pltpu.VMEM((H_F * SAMPLE_ROWS, D_HF), F32), pltpu.VMEM((H_F, 1), F32)]
    grid_spec = pltpu.PrefetchScalarGridSpec(
        num_scalar_prefetch=1, grid=(n_tiles, N_EXPERTS),
        in_specs=in_specs, out_specs=out_specs, scratch_shapes=scratch)
    outs = pl.pallas_call(
        functools.partial(_post_kernel, fox_pages=pages, fox_steps=steps),
        grid_spec=grid_spec,
        out_shape=out_shape,
        compiler_params=_params("arbitrary", "arbitrary"),
        name="post",
    )(table, *operands)
    return outs if fox is not None else outs[0]


def _pad_cols(w, n):
    return jnp.pad(w, ((0, 0), (0, n - w.shape[1])))


def _hi_lo_cols(w):
    hi, lo = _split_bf16(w)
    return jnp.concatenate([hi, lo], axis=1)


def _pad_seq(a):
    nb = a.shape[0] // DEC_SEQ
    return jnp.pad(a.reshape(nb, DEC_SEQ, a.shape[1]), ((0, 0), (0, SAMPLE_ROWS - DEC_SEQ), (0, 0)))


def _unpad_seq(a):
    return a[:, :DEC_SEQ].reshape(a.shape[0] * DEC_SEQ, a.shape[2])


def kernel(x_prompt, x_sample, c_prompt, c_sample, cache_k, cache_v, cache_logf, state_C, state_n, state_m,
           page_table, w_ada, b_ada, g_norm1, g_norm2, w_in, b_ig, b_fg, b_ff, g_out_m, g_out_f, w_out,
           w_group, b_group, w_router, b_router, w_gate, w_up, w_down, g_final):
    depth = w_ada.shape[0]
    assert depth == 1 and x_prompt.shape[0] == 1
    t = x_prompt.shape[1]
    nb, ds = x_sample.shape[:2]
    assert ds == DEC_SEQ
    ts = nb * ds
    xp = x_prompt.reshape(t, D_MODEL)
    xs = x_sample.reshape(ts, D_MODEL)
    l = 0

    w = w_in[l]
    o = np.cumsum([0, W_M, W_M, W_M, W_M, H_M, H_M, W_F, W_F, W_F, H_F])
    w_main = jnp.concatenate([w[:, o[0]:o[4]], w[:, o[6]:o[9]]], axis=1).astype(BF16)
    w_gates = _pad_cols(jnp.concatenate([w[:, o[4]:o[6]], w[:, o[9]:o[10]]], axis=1), GATE_COLS)
    w_gate2 = _hi_lo_cols(w_gates)
    gate_bias = _pad_cols(jnp.concatenate([b_ig[l], b_fg[l], b_ff[l]])[None, :], GATE_COLS)
    gm = g_out_m[l].reshape(1, W_M)
    gf = g_out_f[l].reshape(1, W_F)
    wo_m = w_out[l][:W_M].astype(BF16)
    wo_f = w_out[l][W_M:].astype(BF16)
    w_route = jnp.concatenate(
        [w_group[l], jnp.transpose(w_router[l], (1, 0, 2)).reshape(D_MODEL, N_EXPERTS)], axis=1)
    w_route2 = _hi_lo_cols(_pad_cols(w_route, LANES))
    b_route = _pad_cols(jnp.concatenate([b_group[l], b_router[l].reshape(-1)])[None, :], LANES)
    wg = w_gate[l].reshape(N_EXPERTS, D_MODEL, D_FF_E).astype(BF16)
    wu = w_up[l].reshape(N_EXPERTS, D_MODEL, D_FF_E).astype(BF16)
    wd = w_down[l].reshape(N_EXPERTS, D_FF_E, D_MODEL).astype(BF16)

    rows = -(-(1 + nb) // 8) * 8
    c_all = jnp.pad(jnp.concatenate([c_prompt, c_sample], axis=0), ((0, rows - 1 - nb), (0, 0)))
    mod = _adaln(c_all, w_ada[l], b_ada[l])
    p_mod = [mod[0:1, i * D_MODEL:(i + 1) * D_MODEL] for i in range(6)]
    s_mod = [jnp.repeat(mod[1:1 + nb, i * D_MODEL:(i + 1) * D_MODEL], ds, axis=0) for i in range(6)]
    g1 = g_norm1[l][None, :]
    g2 = g_norm2[l][None, :]
    gfin = g_final[None, :]

    smq, smk, smv, smo, sfq, sfk, sfv, sfk16, sfv16, sgates = _inproj(
        xs, s_mod[0], s_mod[1], g1, w_main, w_gate2, gate_bias, ts)
    m0 = jnp.broadcast_to(state_m[l][:, :, None], (nb, H_M, LANES))
    sym, c_s, n_s, m_s = _mlstm_sample(_pad_seq(smq), _pad_seq(smk), _pad_seq(smv), _pad_seq(smo), _pad_seq(sgates),
                                       gm, state_C[l], state_n[l], m0)
    lf_new = sgates[:, FOX_COL:FOX_COL + H_F]
    lf_new_t = jnp.pad(jnp.transpose(lf_new.reshape(nb, ds, H_F), (0, 2, 1)), ((0, 0), (0, 0), (0, PAGE_SIZE - ds)))
    fox = (page_table, _pad_seq(sfq), _pad_seq(sfk16), _pad_seq(sfv16), lf_new_t, gf,
           jnp.transpose(cache_k[l], (0, 2, 3, 1)), jnp.transpose(cache_v[l], (0, 2, 3, 1)),
           jnp.transpose(cache_logf[l], (0, 2, 1)))

    tm = min(TOKEN_TILE, t)
    tq = min(FOX_BLOCK, t)
    fq, fk, fv, fk16, fv16, gates, ym, fcol, c_p, nm_p = _prompt_mixer(
        xp, p_mod[0], p_mod[1], g1, w_main, w_gate2, gate_bias, gm, tm)
    qat, ka, vat, stats = _fox_prep(fq, fk16, fv16, fcol, tq)
    yf = _fox_prompt(qat, ka, vat, stats, gf.reshape(W_F, 1), tq)
    y_prompt, syf = _post(xp, ym, yf, p_mod[2], p_mod[3], p_mod[4], p_mod[5], g2, gfin,
                          wo_m, wo_f, w_route2, b_route, wg, wu, wd, min(POST_TILE, t), fox=fox)
    y_sample = _post(xs, _unpad_seq(sym), _unpad_seq(syf), s_mod[2], s_mod[3], s_mod[4], s_mod[5], g2, gfin,
                     wo_m, wo_f, w_route2, b_route, wg, wu, wd, ts)

    return (
        y_prompt.reshape(1, t, D_MODEL), y_sample.reshape(nb, ds, D_MODEL),
        fk.reshape(1, 1, t, H_F, D_HF), fv.reshape(1, 1, t, H_F, D_HF),
        gates[:, FOX_COL:FOX_COL + H_F].reshape(1, 1, t, H_F),
        c_p[None, None], nm_p[None, None, :H_M], nm_p[None, None, H_M:, 0],
        sfk.reshape(1, nb, ds, H_F, D_HF), sfv.reshape(1, nb, ds, H_F, D_HF), lf_new.reshape(1, nb, ds, H_F),
        c_s[None], n_s[None], m_s[None, :, :, 0],
    )
```

```python
import functools
import math

import jax
import jax.numpy as jnp
import numpy as np
from jax import lax
from jax.experimental import pallas as pl
from jax.experimental.pallas import tpu as pltpu

F32 = jnp.float32
BF16 = jnp.bfloat16
HIGHEST = lax.Precision.HIGHEST

D_MODEL = 1024
H_M = 4
D_HM = 128
W_M = H_M * D_HM
H_F = 8
D_HF = 64
W_F = H_F * D_HF
PAGE_SIZE = 128
CHUNK = 128
DEC_SEQ = 4
N_GROUPS = 4
EXP_PER_GROUP = 4
N_EXPERTS = N_GROUPS * EXP_PER_GROUP
D_FF_E = D_MODEL // 4
FOX_SCALE = D_HF ** -0.5
LOG2E = math.log2(math.e)
EPS = 1e-6

LANES = 128
BF16_SUBLANES = 16
MXU_DIM = 256
GATE_COLS = LANES
FOX_COL = 2 * H_M
SAMPLE_ROWS = BF16_SUBLANES
AUGW = MXU_DIM
VROWS = D_HF + BF16_SUBLANES
TOKEN_TILE = 512
POST_TILE = 1024
FOX_BLOCK = 1024
Q_TILES = 4
STATS_ROWS = 16
SKIP_BELOW = 160.0
VMEM_LIMIT = 60000 * 1024


def _params(*sem):
    return pltpu.CompilerParams(dimension_semantics=sem, vmem_limit_bytes=VMEM_LIMIT)


def _dot(a, b):
    return jnp.dot(a, b, preferred_element_type=F32)


def _dot_nt(a, b):
    return lax.dot_general(a, b, (((1,), (1,)), ((), ())), preferred_element_type=F32)


def _dot_tn(a, b):
    return lax.dot_general(a, b, (((0,), (0,)), ((), ())), preferred_element_type=F32)


def _dot_hi(a, b):
    return jnp.dot(a, b, preferred_element_type=F32, precision=HIGHEST)


def _rms(x, g):
    return x * lax.rsqrt(jnp.mean(x * x, axis=-1, keepdims=True) + EPS) * g


def _log_sigmoid(x):
    return jnp.minimum(x, 0.0) - jnp.log1p(jnp.exp(-jnp.abs(x)))


def _split_bf16(x):
    hi = x.astype(BF16)
    lo = (x - hi.astype(F32)).astype(BF16)
    return hi, lo


def _split3(x):
    hi = x.astype(BF16).astype(F32)
    mid = (x - hi).astype(BF16).astype(F32)
    lo = (x - hi - mid).astype(BF16).astype(F32)
    return hi, mid, lo


def _adaln_kernel(c_ref, w_ref, b_ref, o_ref):
    c = c_ref[...]
    o_ref[...] = _dot_hi(c * jax.nn.sigmoid(c), w_ref[...]) + b_ref[...]


def _adaln(c_all, w_ada, b_ada):
    rows = c_all.shape[0]
    n = w_ada.shape[1]
    tn = 1536
    return pl.pallas_call(
        _adaln_kernel,
        grid=(n // tn,),
        in_specs=[
            pl.BlockSpec((rows, D_MODEL), lambda j: (0, 0)),
            pl.BlockSpec((D_MODEL, tn), lambda j: (0, j)),
            pl.BlockSpec((1, tn), lambda j: (0, j)),
        ],
        out_specs=pl.BlockSpec((rows, tn), lambda j: (0, j)),
        out_shape=jax.ShapeDtypeStruct((rows, n), F32),
        compiler_params=_params("arbitrary"),
        name="adaln",
    )(c_all, w_ada, b_ada.reshape(1, n))


def _inproj_kernel(x_ref, shift_ref, scale_ref, g_ref, w_ref, wg_ref, gb_ref,
                   mq_ref, mk_ref, mv_ref, mo_ref, fq_ref, fk_ref, fv_ref, fk16_ref, fv16_ref, gate_ref):
    h = _rms(x_ref[...], g_ref[...]) * (1.0 + scale_ref[...]) + shift_ref[...]
    hb, h_lo = _split_bf16(h)

    def proj(j):
        w = w_ref[:, j * W_M:(j + 1) * W_M]
        return _dot_hi(h, w) if w_ref.dtype == F32 else _dot(hb, w)

    mq_ref[...] = proj(0).astype(mq_ref.dtype)
    mk_ref[...] = (proj(1) * (D_HM ** -0.5)).astype(mk_ref.dtype)
    mv_ref[...] = proj(2).astype(mv_ref.dtype)
    mo_ref[...] = jax.nn.sigmoid(proj(3))
    fq_ref[...] = (proj(4) * (FOX_SCALE * LOG2E)).astype(BF16)
    fk = proj(5)
    fk_ref[...] = fk
    fk16_ref[...] = fk.astype(BF16)
    fv = proj(6)
    fv_ref[...] = fv
    fv16_ref[...] = fv.astype(BF16)
    gg = _dot(hb, wg_ref[...])
    gpre = gg[:, :GATE_COLS] + gg[:, GATE_COLS:] + _dot(h_lo, wg_ref[:, :GATE_COLS]) + gb_ref[...]
    lane = lax.broadcasted_iota(jnp.int32, gpre.shape, 1)
    gate_ref[...] = jnp.where(lane < H_M, gpre, jnp.where(lane < FOX_COL + H_F, _log_sigmoid(gpre), 0.0))


def _inproj(x, shift, scale, g_norm, w_main, w_gate2, gate_bias, tm):
    t = x.shape[0]
    mod_rows = shift.shape[0]
    mod_map = (lambda i: (0, 0)) if mod_rows == 1 else (lambda i: (i, 0))
    mod_block = (1, D_MODEL) if mod_rows == 1 else (tm, D_MODEL)
    tok = lambda w: pl.BlockSpec((tm, w), lambda i: (i, 0))
    const = lambda shape: pl.BlockSpec(shape, lambda i: (0, 0))
    out_shapes = (
        jax.ShapeDtypeStruct((t, W_M), w_main.dtype), jax.ShapeDtypeStruct((t, W_M), w_main.dtype),
        jax.ShapeDtypeStruct((t, W_M), w_main.dtype), jax.ShapeDtypeStruct((t, W_M), F32),
        jax.ShapeDtypeStruct((t, W_F), BF16), jax.ShapeDtypeStruct((t, W_F), F32),
        jax.ShapeDtypeStruct((t, W_F), F32), jax.ShapeDtypeStruct((t, W_F), BF16),
        jax.ShapeDtypeStruct((t, W_F), BF16), jax.ShapeDtypeStruct((t, GATE_COLS), F32),
    )
    return pl.pallas_call(
        _inproj_kernel,
        grid=(t // tm,),
        in_specs=[
            tok(D_MODEL),
            pl.BlockSpec(mod_block, mod_map), pl.BlockSpec(mod_block, mod_map),
            const((1, D_MODEL)), const(w_main.shape), const(w_gate2.shape), const((1, GATE_COLS)),
        ],
        out_specs=[tok(W_M)] * 4 + [tok(W_F)] * 5 + [tok(GATE_COLS)],
        out_shape=out_shapes,
        compiler_params=_params("arbitrary"),
        name="inproj",
    )(x, shift, scale, g_norm, w_main, w_gate2, gate_bias)


def _mlstm_chunk(rows, q_ref, k_ref, v_ref, o_ref, g_ref, gout_ref, c_ref, nm_ref, fcarry_ref):
    L = CHUNK
    row = lax.broadcasted_iota(jnp.int32, (L, L), 0)
    col = lax.broadcasted_iota(jnp.int32, (L, L), 1)
    causal = row >= col
    gates = g_ref[rows, :]
    cum = _dot_hi(causal.astype(F32), gates)
    gates_t = gates.T
    cum_t = cum.T
    outs = []
    for h in range(H_M):
        sl = slice(h * D_HM, (h + 1) * D_HM)
        a_row = gates_t[h:h + 1, :] - cum_t[H_M + h:H_M + h + 1, :]
        b_col = cum[:, H_M + h:H_M + h + 1]
        a_col = gates[:, h:h + 1] - b_col
        m_prev = nm_ref[H_M + h:H_M + h + 1, 0:1]
        a_mat = jnp.where(causal, a_row, -jnp.inf)
        r = jnp.maximum(jnp.max(a_mat, axis=-1, keepdims=True), m_prev)
        w_intra = jnp.exp(a_mat - r)
        w_inter = jnp.exp(m_prev - r)
        q = q_ref[rows, sl]
        k = k_ref[rows, sl]
        v = v_ref[rows, sl]
        s = _dot_nt(q, k) * w_intra
        c_old = c_ref[h]
        n_old = nm_ref[h:h + 1, :]
        num = _dot(s.astype(BF16), v) + w_inter * _dot_nt(q, c_old.astype(BF16))
        den = (jnp.sum(s, axis=-1, keepdims=True)
               + w_inter * jnp.sum(q.astype(F32) * n_old, axis=-1, keepdims=True))
        m_t = b_col + r
        hid = num / jnp.maximum(jnp.abs(den), jnp.exp(-m_t))
        m_new = m_t[L - 1:L, :]
        b_last = b_col[L - 1:L, :]
        w_state = jnp.exp(a_col + (b_last - m_new))
        decay = jnp.exp(b_last + m_prev - m_new)
        vw = (v.astype(F32) * w_state).astype(BF16)
        c_ref[h] = decay * c_old + _dot_tn(vw, k)
        nm_ref[h:h + 1, :] = decay * n_old + jnp.sum(k.astype(F32) * w_state, axis=0, keepdims=True)
        nm_ref[H_M + h:H_M + h + 1, :] = jnp.broadcast_to(m_new, (1, LANES))
        outs.append(_rms(o_ref[rows, sl] * hid, gout_ref[:, sl]).astype(BF16))
    f_glob = cum + fcarry_ref[...]
    fcarry_ref[...] = f_glob[L - 1:L, :]
    return jnp.concatenate(outs, axis=1), f_glob


def _prompt_mixer_kernel(x_ref, shift_ref, scale_ref, g_ref, w_ref, wg_ref, gb_ref, gout_ref,
                         fq_ref, fk_ref, fv_ref, fk16_ref, fv16_ref, gate_ref, ym_ref, fcol_ref, c_ref, nm_ref,
                         q_a, k_a, v_a, o_a, g_a, q_b, k_b, v_b, o_b, g_b, fcarry_ref, *, tm):
    i = pl.program_id(0)

    @pl.when(i == 0)
    def _init():
        for ref in (c_ref, nm_ref, fcarry_ref, q_b, k_b, v_b, o_b, g_b):
            ref[...] = jnp.zeros_like(ref)

    def step(write, read):
        wq, wk, wv, wo, wgt = write
        h = _rms(x_ref[...], g_ref[...]) * (1.0 + scale_ref[...]) + shift_ref[...]
        hb, h_lo = _split_bf16(h)

        def proj(j):
            return _dot(hb, w_ref[:, j * W_M:(j + 1) * W_M])

        def scan(c):
            rows = slice(c * CHUNK, (c + 1) * CHUNK)
            ym, f_glob = _mlstm_chunk(rows, *read, gout_ref, c_ref, nm_ref, fcarry_ref)
            ym_ref[rows, :] = ym
            fcol_ref[rows, :] = f_glob

        def stage0():
            wq[...] = proj(0).astype(BF16)
            wk[...] = (proj(1) * (D_HM ** -0.5)).astype(BF16)

        def stage1():
            wv[...] = proj(2).astype(BF16)
            wo[...] = jax.nn.sigmoid(proj(3))

        def stage2():
            fq_ref[...] = (proj(4) * (FOX_SCALE * LOG2E)).astype(BF16)
            fk = proj(5)
            fk_ref[...] = fk
            fk16_ref[...] = fk.astype(BF16)

        def stage3():
            fv = proj(6)
            fv_ref[...] = fv
            fv16_ref[...] = fv.astype(BF16)
            gg = _dot(hb, wg_ref[...])
            gpre = gg[:, :GATE_COLS] + gg[:, GATE_COLS:] + _dot(h_lo, wg_ref[:, :GATE_COLS]) + gb_ref[...]
            lane = lax.broadcasted_iota(jnp.int32, gpre.shape, 1)
            gts = jnp.where(lane < H_M, gpre, jnp.where(lane < FOX_COL + H_F, _log_sigmoid(gpre), 0.0))
            gate_ref[...] = gts
            wgt[...] = gts

        stages = (stage0, stage1, stage2, stage3)
        n_chunks = tm // CHUNK
        for c in range(max(n_chunks, len(stages))):
            if c < len(stages):
                stages[c]()
            if c < n_chunks:
                scan(c)

    buf_a = (q_a, k_a, v_a, o_a, g_a)
    buf_b = (q_b, k_b, v_b, o_b, g_b)

    @pl.when(lax.rem(i, 2) == 0)
    def _even():
        step(buf_a, buf_b)

    @pl.when(lax.rem(i, 2) == 1)
    def _odd():
        step(buf_b, buf_a)


def _prompt_mixer(x, shift, scale, g_norm, w_main, w_gate2, gate_bias, g_out_m, tm):
    t = x.shape[0]
    n = t // tm
    cur = lambda w: pl.BlockSpec((tm, w), lambda i: (jnp.minimum(i, n - 1), 0))
    prev = lambda w: pl.BlockSpec((tm, w), lambda i: (jnp.maximum(i - 1, 0), 0))
    const = lambda shape: pl.BlockSpec(shape, lambda i: (0,) * len(shape))
    tile = lambda w, dt: pltpu.VMEM((tm, w), dt)
    buf = [tile(W_M, BF16), tile(W_M, BF16), tile(W_M, BF16), tile(W_M, F32), tile(GATE_COLS, F32)]
    return pl.pallas_call(
        functools.partial(_prompt_mixer_kernel, tm=tm),
        grid=(n + 1,),
        in_specs=[cur(D_MODEL), const((1, D_MODEL)), const((1, D_MODEL)), const((1, D_MODEL)),
                  const(w_main.shape), const(w_gate2.shape), const((1, GATE_COLS)), const((1, W_M))],
        out_specs=[cur(W_F)] * 5 + [cur(GATE_COLS), prev(W_M), prev(GATE_COLS),
                                    const((H_M, D_HM, D_HM)), const((2 * H_M, D_HM))],
        out_shape=(
            jax.ShapeDtypeStruct((t, W_F), BF16), jax.ShapeDtypeStruct((t, W_F), F32),
            jax.ShapeDtypeStruct((t, W_F), F32), jax.ShapeDtypeStruct((t, W_F), BF16),
            jax.ShapeDtypeStruct((t, W_F), BF16), jax.ShapeDtypeStruct((t, GATE_COLS), F32),
            jax.ShapeDtypeStruct((t, W_M), BF16), jax.ShapeDtypeStruct((t, GATE_COLS), F32),
            jax.ShapeDtypeStruct((H_M, D_HM, D_HM), F32), jax.ShapeDtypeStruct((2 * H_M, D_HM), F32),
        ),
        scratch_shapes=buf + buf + [pltpu.VMEM((1, GATE_COLS), F32)],
        compiler_params=_params("arbitrary"),
        name="prompt_mixer",
    )(x, shift, scale, g_norm, w_main, w_gate2, gate_bias, g_out_m)


def _fox_prep_kernel(q_ref, k_ref, v_ref, fcol_ref, qat_ref, ka_ref, vat_ref, stats_ref):
    tm = q_ref.shape[0]
    lane = lax.broadcasted_iota(jnp.int32, (tm, LANES), 1)
    lo = lane < D_HF
    f_all = fcol_ref[...] * LOG2E
    ones_rows = jnp.where(lax.broadcasted_iota(jnp.int32, (VROWS - D_HF, tm), 0) == 0, 1.0, 0.0).astype(BF16)
    half_of_lane = lax.broadcasted_iota(jnp.int32, (LANES, LANES), 0) // D_HF
    half_sum = (half_of_lane == lax.broadcasted_iota(jnp.int32, (LANES, LANES), 1)).astype(BF16)
    q_norms, k_norms = [], []
    for p in range(H_F // 2):
        sl = slice(p * LANES, (p + 1) * LANES)
        q2 = q_ref[:, sl].astype(F32)
        k2 = k_ref[:, sl].astype(F32)
        q_norms.append(jnp.max(_dot((q2 * q2).astype(BF16), half_sum), axis=0, keepdims=True))
        k_norms.append(jnp.max(_dot((k2 * k2).astype(BF16), half_sum), axis=0, keepdims=True))
        k_aug = jnp.zeros((tm, LANES), F32)
        for half in range(2):
            h = 2 * p + half
            hi, mid, lw = _split3(f_all[:, FOX_COL + h:FOX_COL + h + 1])
            b = 8 * half
            q_aug = jnp.where(lane == b, hi, jnp.where(lane == b + 1, mid, jnp.where(lane == b + 2, lw, jnp.where(
                (lane >= b + 3) & (lane < b + 6), 1.0, 0.0))))
            k_aug = k_aug + jnp.where((lane >= b) & (lane < b + 3), 1.0, jnp.where(
                lane == b + 3, -hi, jnp.where(lane == b + 4, -mid, jnp.where(lane == b + 5, -lw, 0.0))))
            q_own = jnp.where(lo, q2, 0.0) if half == 0 else jnp.where(lo, 0.0, q2)
            qat_ref[h * AUGW:h * AUGW + LANES, :] = q_own.T.astype(BF16)
            qat_ref[h * AUGW + LANES:(h + 1) * AUGW, :] = q_aug.T.astype(BF16)
        ka_ref[:, p * AUGW:p * AUGW + LANES] = k_ref[:, sl]
        ka_ref[:, p * AUGW + LANES:(p + 1) * AUGW] = k_aug.astype(BF16)
        vt = v_ref[:, sl].astype(F32).T.astype(BF16)
        for half in range(2):
            h = 2 * p + half
            vat_ref[h * VROWS:h * VROWS + D_HF, :] = vt[half * D_HF:(half + 1) * D_HF, :]
            vat_ref[h * VROWS + D_HF:(h + 1) * VROWS, :] = ones_rows
    pad = jnp.zeros((STATS_ROWS - H_F - 2, LANES), F32)
    stats_ref[0] = jnp.concatenate(q_norms + k_norms + [f_all[0:1, :], f_all[tm - 1:tm, :], pad], axis=0)


def _fox_prep(fq, fk16, fv16, fcol, tm):
    t = fq.shape[0]
    tok = lambda w: pl.BlockSpec((tm, w), lambda i: (i, 0))
    tr = lambda r: pl.BlockSpec((r, tm), lambda i: (0, i))
    return pl.pallas_call(
        _fox_prep_kernel,
        grid=(t // tm,),
        in_specs=[tok(W_F), tok(W_F), tok(W_F), tok(GATE_COLS)],
        out_specs=[tr(H_F * AUGW), tok(H_F // 2 * AUGW), tr(H_F * VROWS),
                   pl.BlockSpec((1, STATS_ROWS, LANES), lambda i: (i, 0, 0))],
        out_shape=(jax.ShapeDtypeStruct((H_F * AUGW, t), BF16), jax.ShapeDtypeStruct((t, H_F // 2 * AUGW), BF16),
                   jax.ShapeDtypeStruct((H_F * VROWS, t), BF16), jax.ShapeDtypeStruct((t // tm, STATS_ROWS, LANES), F32)),
        compiler_params=_params("arbitrary"),
        name="fox_prep",
    )(fq, fk16, fv16, fcol)


def _fox_prompt_kernel(qtab_ref, ktab_ref, live_ref, kfetch_ref, qat_ref, ka_ref, vat_ref, gcol_ref,
                       y_ref, m_scr, acc_scr, *, tq):
    i = pl.program_id(0)
    qb = qtab_ref[i]
    kb = ktab_ref[i]

    @pl.when(kb == 0)
    def _init():
        m_scr[...] = jnp.full_like(m_scr, -jnp.inf)
        acc_scr[...] = jnp.zeros_like(acc_scr)

    def head_update(h, diagonal):
        p = h // 2
        cq = tq // Q_TILES
        scores = []
        for c in range(Q_TILES):
            nk = (c + 1) * cq if diagonal else tq
            s = _dot(ka_ref[0:nk, p * AUGW:(p + 1) * AUGW], qat_ref[h * AUGW:(h + 1) * AUGW, c * cq:(c + 1) * cq])
            if diagonal:
                keep = (lax.broadcasted_iota(jnp.int32, (nk, cq), 0)
                        <= lax.broadcasted_iota(jnp.int32, (nk, cq), 1) + c * cq)
                s = jnp.where(keep, s, -jnp.inf)
            scores.append(s)
        for c in range(Q_TILES):
            cols = slice(c * cq, (c + 1) * cq)
            s = scores[c]
            m_prev = m_scr[h, :, cols]
            m_new = jnp.maximum(m_prev, jnp.max(s, axis=0, keepdims=True))
            m_scr[h, :, cols] = m_new
            prob_t = jnp.exp2(s - m_new).astype(BF16)
            acc_scr[h, :, cols] = (jnp.exp2(m_prev - m_new) * acc_scr[h, :, cols]
                                   + _dot(vat_ref[h * VROWS:(h + 1) * VROWS, 0:s.shape[0]], prob_t))

    @pl.when(kb < qb)
    def _off_diagonal():
        for h in range(H_F):
            @pl.when(live_ref[i * H_F + h] != 0)
            def _live_head():
                head_update(h, False)

    @pl.when(kb == qb)
    def _diagonal():
        for h in range(H_F):
            head_update(h, True)
        for p in range(H_F // 2):
            outs = []
            for h in (2 * p, 2 * p + 1):
                acc = acc_scr[h]
                out = acc[:D_HF, :] / acc[D_HF:D_HF + 1, :]
                ms = jnp.mean(out * out, axis=0, keepdims=True)
                outs.append(out * lax.rsqrt(ms + EPS) * gcol_ref[h * D_HF:(h + 1) * D_HF, :])
            y_ref[:, p * LANES:(p + 1) * LANES] = jnp.concatenate(outs, axis=0).T.astype(BF16)


def _fox_live(stats, qtab, ktab):
    slack = 1.0 + 2.0 ** -6
    qn = jnp.sqrt(stats[:, 0:H_F // 2, 0:2].reshape(-1, H_F) * slack)
    kn = jnp.sqrt(stats[:, H_F // 2:H_F, 0:2].reshape(-1, H_F) * slack)
    f_first = stats[:, H_F, FOX_COL:FOX_COL + H_F]
    f_last = stats[:, H_F + 1, FOX_COL:FOX_COL + H_F]
    bound = qn[qtab] * (kn[ktab] + kn[qtab]) + f_first[qtab] - f_last[ktab]
    dead = (bound < -SKIP_BELOW) & (ktab < qtab - 1)[:, None]
    return jnp.logical_not(dead).astype(jnp.int32).reshape(-1)


def _fox_prompt(qat, ka, vat, stats, g_col, tq):
    t = ka.shape[0]
    nb = t // tq
    pairs = [(q, k) for q in range(nb) for k in range(q + 1)]
    qtab = np.array([p[0] for p in pairs], np.int32)
    ktab = np.array([p[1] for p in pairs], np.int32)
    live = _fox_live(stats, qtab, ktab)
    fetch = jnp.any(live.reshape(len(pairs), H_F) != 0, axis=1) | (ktab == qtab)
    kfetch = jnp.asarray(ktab)[lax.cummax(jnp.where(fetch, jnp.arange(len(pairs)), 0))]
    grid_spec = pltpu.PrefetchScalarGridSpec(
        num_scalar_prefetch=4,
        grid=(len(pairs),),
        in_specs=[
            pl.BlockSpec((H_F * AUGW, tq), lambda i, qt, kt, lv, kf: (0, qt[i])),
            pl.BlockSpec((tq, H_F // 2 * AUGW), lambda i, qt, kt, lv, kf: (kf[i], 0)),
            pl.BlockSpec((H_F * VROWS, tq), lambda i, qt, kt, lv, kf: (0, kf[i])),
            pl.BlockSpec((W_F, 1), lambda i, qt, kt, lv, kf: (0, 0)),
        ],
        out_specs=pl.BlockSpec((tq, W_F), lambda i, qt, kt, lv, kf: (qt[i], 0)),
        scratch_shapes=[pltpu.VMEM((H_F, 1, tq), F32), pltpu.VMEM((H_F, VROWS, tq), F32)],
    )
    return pl.pallas_call(
        functools.partial(_fox_prompt_kernel, tq=tq),
        grid_spec=grid_spec,
        out_shape=jax.ShapeDtypeStruct((t, W_F), BF16),
        compiler_params=_params("arbitrary"),
        name="fox_prompt",
    )(jnp.asarray(qtab), jnp.asarray(ktab), live, kfetch, qat, ka, vat, g_col)


def _mlstm_sample_kernel(q_ref, k_ref, v_ref, o_ref, g_ref, gout_ref, c0_ref, n0_ref, m0_ref,
                         ym_ref, c_ref, n_ref, m_ref):
    R = SAMPLE_ROWS
    gates = g_ref[0]
    rows = [gates[0:1, :]]
    for t in range(1, DEC_SEQ):
        rows.append(rows[-1] + gates[t:t + 1, :])
    rows += [rows[-1]] * (R - DEC_SEQ)
    cum = jnp.concatenate(rows, axis=0)
    ridx = lax.broadcasted_iota(jnp.int32, (R, 1), 0)
    for h in range(H_M):
        sl = slice(h * D_HM, (h + 1) * D_HM)
        b_col = cum[:, H_M + h:H_M + h + 1]
        a_col = gates[:, h:h + 1] - b_col
        m_prev = m0_ref[0, h:h + 1, 0:1]
        cm = jnp.full((R, 1), -jnp.inf, F32)
        for s in range(DEC_SEQ):
            cm = jnp.maximum(cm, jnp.where(ridx >= s, a_col[s:s + 1, :], -jnp.inf))
        r = jnp.maximum(cm, m_prev)
        w_inter = jnp.exp(m_prev - r)
        q = q_ref[0, :, sl]
        k = k_ref[0, :, sl]
        v = v_ref[0, :, sl]
        q32, k32, v32 = q.astype(F32), k.astype(F32), v.astype(F32)
        c_old = c0_ref[0, h]
        n_old = n0_ref[0, h:h + 1, :]
        if q.dtype == F32:
            inter = lax.dot_general(q, c_old, (((1,), (1,)), ((), ())), preferred_element_type=F32, precision=HIGHEST)
        else:
            inter = _dot_nt(q, c_old.astype(BF16))
        num = w_inter * inter
        den = w_inter * jnp.sum(q32 * n_old, axis=-1, keepdims=True)
        for s in range(DEC_SEQ):
            w_s = jnp.where(ridx >= s, jnp.exp(a_col[s:s + 1, :] - r), 0.0)
            s_col = jnp.sum(q32 * k32[s:s + 1, :], axis=-1, keepdims=True) * w_s
            num = num + s_col * v32[s:s + 1, :]
            den = den + s_col
        m_t = b_col + r
        hid = num / jnp.maximum(jnp.abs(den), jnp.exp(-m_t))
        m_new = m_t[DEC_SEQ - 1:DEC_SEQ, :]
        b_last = b_col[DEC_SEQ - 1:DEC_SEQ, :]
        w_state = jnp.where(ridx < DEC_SEQ, jnp.exp(a_col + (b_last - m_new)), 0.0)
        decay = jnp.exp(b_last + m_prev - m_new)
        pad = jnp.zeros((D_HM - R, D_HM), BF16)
        vw = jnp.concatenate([(v32 * w_state).astype(BF16), pad], axis=0)
        kp = jnp.concatenate([k.astype(BF16), pad], axis=0)
        c_ref[0, h] = decay * c_old + _dot_tn(vw, kp)
        n_ref[0, h:h + 1, :] = decay * n_old + jnp.sum(k32 * w_state, axis=0, keepdims=True)
        m_ref[0, h:h + 1, :] = jnp.broadcast_to(m_new, (1, LANES))
        ym_ref[0, :, sl] = _rms(o_ref[0, :, sl] * hid, gout_ref[:, sl]).astype(ym_ref.dtype)


def _mlstm_sample(mq, mk, mv, mo, gates, g_out_m, c0, n0, m0):
    nb = mq.shape[0]
    seq = lambda w: pl.BlockSpec((1, SAMPLE_ROWS, w), lambda b: (b, 0, 0))
    st_c = pl.BlockSpec((1, H_M, D_HM, D_HM), lambda b: (b, 0, 0, 0))
    st_v = pl.BlockSpec((1, H_M, D_HM), lambda b: (b, 0, 0))
    return pl.pallas_call(
        _mlstm_sample_kernel,
        grid=(nb,),
        in_specs=[seq(W_M), seq(W_M), seq(W_M), seq(W_M), seq(GATE_COLS),
                  pl.BlockSpec((1, W_M), lambda b: (0, 0)), st_c, st_v, st_v],
        out_specs=[seq(W_M), st_c, st_v, st_v],
        out_shape=(
            jax.ShapeDtypeStruct((nb, SAMPLE_ROWS, W_M), F32),
            jax.ShapeDtypeStruct((nb, H_M, D_HM, D_HM), F32),
            jax.ShapeDtypeStruct((nb, H_M, D_HM), F32), jax.ShapeDtypeStruct((nb, H_M, D_HM), F32),
        ),
        compiler_params=_params("arbitrary"),
        name="mlstm_sample",
    )(mq, mk, mv, mo, gates, g_out_m, c0, n0, m0)


def _fox_sample_step(step, n_steps, q_ref, knew_ref, vnew_ref, lfnew_ref, gout_ref, k_refs, v_refs, lf_refs,
                     y_ref, m_scr, l_scr, acc_scr, carry_scr, alongside):
    pages = len(k_refs)
    r = SAMPLE_ROWS

    @pl.when(step == 0)
    def _init():
        m_scr[...] = jnp.full_like(m_scr, -jnp.inf)
        l_scr[...] = jnp.zeros_like(l_scr)
        acc_scr[...] = jnp.zeros_like(acc_scr)
        carry_scr[...] = jnp.zeros_like(carry_scr)

    def absorb(scores, pv):
        s = jnp.concatenate(scores, axis=0)
        m_prev = m_scr[...]
        m_new = jnp.maximum(m_prev, jnp.max(s, axis=-1, keepdims=True))
        alpha = jnp.exp2(m_prev - m_new)
        prob = jnp.exp2(s - m_new)
        l_scr[...] = alpha * l_scr[...] + jnp.sum(prob, axis=-1, keepdims=True)
        pb = prob.astype(BF16)
        acc_scr[...] = alpha * acc_scr[...] + jnp.concatenate(
            [pv(h, pb[h * r:(h + 1) * r, :]) for h in range(H_F)], axis=0)
        m_scr[...] = m_new

    upper = (lax.broadcasted_iota(jnp.int32, (PAGE_SIZE, PAGE_SIZE), 0)
             <= lax.broadcasted_iota(jnp.int32, (PAGE_SIZE, PAGE_SIZE), 1)).astype(F32)
    cum = _dot_hi(jnp.concatenate([lf_refs[j][...] for j in range(pages)], axis=0), upper)
    carry = carry_scr[...]
    f_past = []
    for j in range(pages):
        f_past.append(carry + cum[j * H_F:(j + 1) * H_F, :])
        carry = f_past[-1][:, PAGE_SIZE - 1:PAGE_SIZE]
    carry_scr[...] = carry
    heads = [slice(h * D_HF, (h + 1) * D_HF) for h in range(H_F)]
    scores, v_t = [], []
    for h in range(H_F):
        k_t = jnp.concatenate([k_refs[j][h] for j in range(pages)], axis=1).astype(BF16)
        v_t.append(jnp.concatenate([v_refs[j][h] for j in range(pages)], axis=1).astype(BF16))
        bias = jnp.concatenate([f_past[j][h:h + 1, :] for j in range(pages)], axis=1) * LOG2E
        scores.append(_dot(q_ref[0, :, heads[h]], k_t) - bias)
    alongside()
    absorb(scores, lambda h, prob: _dot_nt(prob, v_t[h]))

    @pl.when(step == n_steps - 1)
    def _finish():
        cn = _dot_hi(lfnew_ref[0], upper)[:, :r]
        causal = lax.broadcasted_iota(jnp.int32, (r, r), 1) <= lax.broadcasted_iota(jnp.int32, (r, r), 0)
        scores = []
        for h in range(H_F):
            bias = (cn[h:h + 1, :] + carry[h:h + 1, :]) * LOG2E
            scores.append(jnp.where(causal, _dot_nt(q_ref[0, :, heads[h]], knew_ref[0, :, heads[h]]) - bias, -jnp.inf))
        absorb(scores, lambda h, prob: _dot(prob, vnew_ref[0, :, heads[h]]))
        out = acc_scr[...] / l_scr[...]
        y_ref[0] = jnp.concatenate(
            [_rms(out[h * r:(h + 1) * r, :], gout_ref[:, heads[h]]) for h in range(H_F)], axis=1).astype(y_ref.dtype)


N_POST_IN = 16
N_FOX_FIXED_IN = 5


def _post_kernel(pt_ref, *refs, fox_pages, fox_steps):
    (x_ref, ym_ref, yf_ref, gt1_ref, sh2_ref, sc2_ref, gt2_ref, g2_ref, gfin_ref,
     wom_ref, wof_ref, wr_ref, br_ref, wg_ref, wu_ref, wd_ref) = refs[:N_POST_IN]
    n_fox_in = N_FOX_FIXED_IN + 3 * fox_pages if fox_pages else 0
    fox_in = refs[N_POST_IN:N_POST_IN + n_fox_in]
    outs = refs[N_POST_IN + n_fox_in:]
    if fox_pages:
        o_ref, y_ref, xp_scr, hb_scr, comb_scr, acc_scr = outs[:6]
        fox_scr = outs[6:]
    else:
        o_ref, xp_scr, hb_scr, comb_scr, acc_scr = outs
    e = pl.program_id(1)

    @pl.when(e == 0)
    def _prepare():
        if wom_ref.dtype == F32:
            mix = _dot_hi(ym_ref[...], wom_ref[...]) + _dot_hi(yf_ref[...], wof_ref[...])
        else:
            mix = _dot(ym_ref[...], wom_ref[...]) + _dot(yf_ref[...], wof_ref[...])
        xp = x_ref[...] + gt1_ref[...] * mix
        xp_scr[...] = xp
        h = _rms(xp, g2_ref[...]) * (1.0 + sc2_ref[...]) + sh2_ref[...]
        hb, h_lo = _split_bf16(h)
        hb_scr[...] = hb
        rr = _dot(hb, wr_ref[...])
        logits = rr[:, :LANES] + rr[:, LANES:] + _dot(h_lo, wr_ref[:, :LANES]) + br_ref[...]
        lane = lax.broadcasted_iota(jnp.int32, logits.shape, 1)
        big = jnp.int32(4 * LANES)
        gl = jnp.where(lane < N_GROUPS, logits, -jnp.inf)
        gmax = jnp.max(gl, axis=-1, keepdims=True)
        gidx = jnp.min(jnp.where(gl == gmax, lane, big), axis=-1, keepdims=True)
        g_w = 1.0 / jnp.sum(jnp.exp(gl - gmax), axis=-1, keepdims=True)
        first = N_GROUPS + EXP_PER_GROUP * gidx
        el = jnp.where((lane >= first) & (lane < first + EXP_PER_GROUP), logits, -jnp.inf)
        emax = jnp.max(el, axis=-1, keepdims=True)
        pe = jnp.exp(el - emax)
        prob = pe / jnp.sum(pe, axis=-1, keepdims=True)
        v1 = jnp.max(prob, axis=-1, keepdims=True)
        i1 = jnp.min(jnp.where(prob == v1, lane, big), axis=-1, keepdims=True)
        rest = jnp.where((lane == i1) | (el == -jnp.inf), -1.0, prob)
        v2 = jnp.max(rest, axis=-1, keepdims=True)
        i2 = jnp.min(jnp.where(rest == v2, lane, big), axis=-1, keepdims=True)
        tot = v1 + v2
        comb_scr[...] = jnp.where(lane == i1, v1 / tot, jnp.where(lane == i2, v2 / tot, 0.0)) * g_w
        acc_scr[...] = jnp.zeros_like(acc_scr)

    def expert():
        hb = hb_scr[...]
        a = _dot(hb, wg_ref[0])
        u = _dot(hb, wu_ref[0])
        lane = lax.broadcasted_iota(jnp.int32, comb_scr.shape, 1)
        ce = jnp.sum(jnp.where(lane == e + N_GROUPS, comb_scr[...], 0.0), axis=-1, keepdims=True)
        act = (a * jax.nn.sigmoid(a) * u * ce).astype(BF16)
        acc_scr[...] += _dot(act, wd_ref[0])

    if fox_pages:
        step = lax.rem(pl.program_id(0) * N_EXPERTS + e, fox_steps)
        p = fox_pages
        _fox_sample_step(step, fox_steps, *fox_in[:N_FOX_FIXED_IN],
                         fox_in[N_FOX_FIXED_IN:N_FOX_FIXED_IN + p], fox_in[N_FOX_FIXED_IN + p:N_FOX_FIXED_IN + 2 * p],
                         fox_in[N_FOX_FIXED_IN + 2 * p:], y_ref, *fox_scr, alongside=expert)
    else:
        expert()

    @pl.when(e == N_EXPERTS - 1)
    def _finish():
        o_ref[...] = _rms(xp_scr[...] + gt2_ref[...] * acc_scr[...], gfin_ref[...])


def _post(x, ym, yf, gt1, sh2, sc2, gt2, g_norm2, g_final, wo_m, wo_f, w_route2, b_route, wg, wu, wd, tm, fox=None):
    t = x.shape[0]
    n_tiles = t // tm
    mod_rows = gt1.shape[0]
    mod_map = (lambda i, e, pt: (0, 0)) if mod_rows == 1 else (lambda i, e, pt: (i, 0))
    mod = pl.BlockSpec((1, D_MODEL) if mod_rows == 1 else (tm, D_MODEL), mod_map)
    tok = lambda w: pl.BlockSpec((tm, w), lambda i, e, pt: (i, 0))
    const = lambda shape: pl.BlockSpec(shape, lambda i, e, pt: (0,) * len(shape), pipeline_mode=pl.Buffered(1))
    expert = lambda shape: pl.BlockSpec((1,) + shape, lambda i, e, pt: (e, 0, 0))
    in_specs = [
        tok(D_MODEL), tok(W_M), tok(W_F), mod, mod, mod, mod,
        const((1, D_MODEL)), const((1, D_MODEL)),
        const(wo_m.shape), const(wo_f.shape), const(w_route2.shape), const((1, LANES)),
        expert((D_MODEL, D_FF_E)), expert((D_MODEL, D_FF_E)), expert((D_FF_E, D_MODEL)),
    ]
    operands = [x, ym, yf, gt1, sh2, sc2, gt2, g_norm2, g_final, wo_m, wo_f, w_route2, b_route, wg, wu, wd]
    out_specs = [tok(D_MODEL)]
    out_shape = [jax.ShapeDtypeStruct((t, D_MODEL), F32)]
    scratch = [pltpu.VMEM((tm, D_MODEL), F32), pltpu.VMEM((tm, D_MODEL), BF16),
               pltpu.VMEM((tm, LANES), F32), pltpu.VMEM((tm, D_MODEL), F32)]
    pages = steps = 0
    table = jnp.zeros((1,), jnp.int32)
    if fox is not None:
        page_table, fq, fk16, fv16, lf_new_t, g_out_f, cache_k, cache_v, cache_lf_t = fox
        nb, n_pages = page_table.shape
        pages, rem = divmod(nb * n_pages, n_tiles * N_EXPERTS)
        assert rem == 0 and pages > 0 and n_pages % pages == 0, (nb, n_pages, n_tiles)
        steps = n_pages // pages
        table = page_table.reshape(-1)
        seq_of = lambda i, e: (i * N_EXPERTS + e) // steps
        seq = lambda w: pl.BlockSpec((1, SAMPLE_ROWS, w), lambda i, e, pt: (seq_of(i, e), 0, 0))

        def page(shape, j):
            def index(i, e, pt):
                f = i * N_EXPERTS + e
                return (pt[(f // steps) * n_pages + (f % steps) * pages + j],) + (0,) * len(shape)
            return pl.BlockSpec((None,) + shape, index)

        in_specs += ([seq(W_F), seq(W_F), seq(W_F),
                      pl.BlockSpec((1, H_F, PAGE_SIZE), lambda i, e, pt: (seq_of(i, e), 0, 0)),
                      const((1, W_F))]
                     + [page((H_F, D_HF, PAGE_SIZE), j) for j in range(pages)] * 2
                     + [page((H_F, PAGE_SIZE), j) for j in range(pages)])
        operands += [fq, fk16, fv16, lf_new_t, g_out_f] + [cache_k] * pages + [cache_v] * pages + [cache_lf_t] * pages
        out_specs.append(seq(W_F))
        out_shape.append(jax.ShapeDtypeStruct((nb, SAMPLE_ROWS, W_F), F32))
        scratch += [pltpu.VMEM((H_F * SAMPLE_ROWS, 1), F32), pltpu.VMEM((H_F * SAMPLE_ROWS, 1), F32),
                    pltpu.VMEM((H_F * SAMPLE_ROWS, D_HF), F32), pltpu.VMEM((H_F, 1), F32)]
    grid_spec = pltpu.PrefetchScalarGridSpec(
        num_scalar_prefetch=1, grid=(n_tiles, N_EXPERTS),
        in_specs=in_specs, out_specs=out_specs, scratch_shapes=scratch)
    outs = pl.pallas_call(
        functools.partial(_post_kernel, fox_pages=pages, fox_steps=steps),
        grid_spec=grid_spec,
        out_shape=out_shape,
        compiler_params=_params("arbitrary", "arbitrary"),
        name="post",
    )(table, *operands)
    return outs if fox is not None else outs[0]


def _pad_cols(w, n):
    return jnp.pad(w, ((0, 0), (0, n - w.shape[1])))


def _hi_lo_cols(w):
    hi, lo = _split_bf16(w)
    return jnp.concatenate([hi, lo], axis=1)


def _pad_seq(a):
    nb = a.shape[0] // DEC_SEQ
    return jnp.pad(a.reshape(nb, DEC_SEQ, a.shape[1]), ((0, 0), (0, SAMPLE_ROWS - DEC_SEQ), (0, 0)))


def _unpad_seq(a):
    return a[:, :DEC_SEQ].reshape(a.shape[0] * DEC_SEQ, a.shape[2])


def kernel(x_prompt, x_sample, c_prompt, c_sample, cache_k, cache_v, cache_logf, state_C, state_n, state_m,
           page_table, w_ada, b_ada, g_norm1, g_norm2, w_in, b_ig, b_fg, b_ff, g_out_m, g_out_f, w_out,
           w_group, b_group, w_router, b_router, w_gate, w_up, w_down, g_final):
    depth = w_ada.shape[0]
    assert depth == 1 and x_prompt.shape[0] == 1
    t = x_prompt.shape[1]
    nb, ds = x_sample.shape[:2]
    assert ds == DEC_SEQ
    ts = nb * ds
    xp = x_prompt.reshape(t, D_MODEL)
    xs = x_sample.reshape(ts, D_MODEL)
    l = 0

    w = w_in[l]
    o = np.cumsum([0, W_M, W_M, W_M, W_M, H_M, H_M, W_F, W_F, W_F, H_F])
    w_main32 = jnp.concatenate([w[:, o[0]:o[4]], w[:, o[6]:o[9]]], axis=1)
    w_main = w_main32.astype(BF16)
    w_gates = _pad_cols(jnp.concatenate([w[:, o[4]:o[6]], w[:, o[9]:o[10]]], axis=1), GATE_COLS)
    w_gate2 = _hi_lo_cols(w_gates)
    gate_bias = _pad_cols(jnp.concatenate([b_ig[l], b_fg[l], b_ff[l]])[None, :], GATE_COLS)
    gm = g_out_m[l].reshape(1, W_M)
    gf = g_out_f[l].reshape(1, W_F)
    wo_m = w_out[l][:W_M].astype(BF16)
    wo_f = w_out[l][W_M:].astype(BF16)
    w_route = jnp.concatenate(
        [w_group[l], jnp.transpose(w_router[l], (1, 0, 2)).reshape(D_MODEL, N_EXPERTS)], axis=1)
    w_route2 = _hi_lo_cols(_pad_cols(w_route, LANES))
    b_route = _pad_cols(jnp.concatenate([b_group[l], b_router[l].reshape(-1)])[None, :], LANES)
    wg = w_gate[l].reshape(N_EXPERTS, D_MODEL, D_FF_E).astype(BF16)
    wu = w_up[l].reshape(N_EXPERTS, D_MODEL, D_FF_E).astype(BF16)
    wd = w_down[l].reshape(N_EXPERTS, D_FF_E, D_MODEL).astype(BF16)

    rows = -(-(1 + nb) // 8) * 8
    c_all = jnp.pad(jnp.concatenate([c_prompt, c_sample], axis=0), ((0, rows - 1 - nb), (0, 0)))
    mod = _adaln(c_all, w_ada[l], b_ada[l])
    p_mod = [mod[0:1, i * D_MODEL:(i + 1) * D_MODEL] for i in range(6)]
    s_mod = [jnp.repeat(mod[1:1 + nb, i * D_MODEL:(i + 1) * D_MODEL], ds, axis=0) for i in range(6)]
    g1 = g_norm1[l][None, :]
    g2 = g_norm2[l][None, :]
    gfin = g_final[None, :]

    smq, smk, smv, smo, sfq, sfk, sfv, sfk16, sfv16, sgates = _inproj(
        xs, s_mod[0], s_mod[1], g1, w_main32, w_gate2, gate_bias, ts)
    m0 = jnp.broadcast_to(state_m[l][:, :, None], (nb, H_M, LANES))
    sym, c_s, n_s, m_s = _mlstm_sample(_pad_seq(smq), _pad_seq(smk), _pad_seq(smv), _pad_seq(smo), _pad_seq(sgates),
                                       gm, state_C[l], state_n[l], m0)
    lf_new = sgates[:, FOX_COL:FOX_COL + H_F]
    lf_new_t = jnp.pad(jnp.transpose(lf_new.reshape(nb, ds, H_F), (0, 2, 1)), ((0, 0), (0, 0), (0, PAGE_SIZE - ds)))
    fox = (page_table, _pad_seq(sfq), _pad_seq(sfk16), _pad_seq(sfv16), lf_new_t, gf,
           jnp.transpose(cache_k[l], (0, 2, 3, 1)), jnp.transpose(cache_v[l], (0, 2, 3, 1)),
           jnp.transpose(cache_logf[l], (0, 2, 1)))

    tm = min(TOKEN_TILE, t)
    tq = min(FOX_BLOCK, t)
    fq, fk, fv, fk16, fv16, gates, ym, fcol, c_p, nm_p = _prompt_mixer(
        xp, p_mod[0], p_mod[1], g1, w_main, w_gate2, gate_bias, gm, tm)
    qat, ka, vat, stats = _fox_prep(fq, fk16, fv16, fcol, tq)
    yf = _fox_prompt(qat, ka, vat, stats, gf.reshape(W_F, 1), tq)
    y_prompt, syf = _post(xp, ym, yf, p_mod[2], p_mod[3], p_mod[4], p_mod[5], g2, gfin,
                          wo_m, wo_f, w_route2, b_route, wg, wu, wd, min(POST_TILE, t), fox=fox)
    y_sample = _post(xs, _unpad_seq(sym), _unpad_seq(syf), s_mod[2], s_mod[3], s_mod[4], s_mod[5], g2, gfin,
                     w_out[l][:W_M], w_out[l][W_M:], w_route2, b_route, wg, wu, wd, ts)

    return (
        y_prompt.reshape(1, t, D_MODEL), y_sample.reshape(nb, ds, D_MODEL),
        fk.reshape(1, 1, t, H_F, D_HF), fv.reshape(1, 1, t, H_F, D_HF),
        gates[:, FOX_COL:FOX_COL + H_F].reshape(1, 1, t, H_F),
        c_p[None, None], nm_p[None, None, :H_M], nm_p[None, None, H_M:, 0],
        sfk.reshape(1, nb, ds, H_F, D_HF), sfv.reshape(1, nb, ds, H_F, D_HF), lf_new.reshape(1, nb, ds, H_F),
        c_s[None], n_s[None], m_s[None, :, :, 0],
    )
```
